```python
import math
import jax, jax.numpy as jnp
from jax import lax
import numpy as np


D_MODEL = 1024
BATCH = 16
SEQ = 2048
DEPTH = 2

N_MEM = 256
XA_HEADS = 4
XA_HEAD_DIM = D_MODEL // XA_HEADS

POOL_WINDOWS = (2, 4, 8, 16)
POOL_GROUP_DIM = D_MODEL // 8
POOL_WIDTH = len(POOL_WINDOWS) * POOL_GROUP_DIM

SSM_HEAD_DIM = 64
SSM_INNER = D_MODEL
SSM_HEADS = SSM_INNER // SSM_HEAD_DIM
SSM_GROUPS = 2
SSM_STATE = 128
SSM_CONV = 4
SSM_CHUNK = 128
SSM_GN = SSM_GROUPS * SSM_STATE
SSM_CONV_DIM = SSM_INNER + 2 * SSM_GN

AB_IN = POOL_WIDTH + SSM_INNER + SSM_CONV_DIM + SSM_HEADS
AB_OUT = POOL_WIDTH + SSM_INNER

CONF_DIM = D_MODEL
CONF_KERNEL = 31
SC_DIM = D_MODEL
SC_KERNEL = 3
CD_IN = 2 * CONF_DIM + 3 * SC_DIM
CD_OUT = CONF_DIM + SC_DIM

MLP_HIDDEN = 4 * D_MODEL
N_EVEN = (DEPTH + 1) // 2
N_ODD = DEPTH // 2
RMS_EPS = 1e-6
LN_EPS = 1e-5

kernel_name = 'hybrid_pool_ssd_conformer_shortconv'


def rmsnorm(x, g):
    xf = x.astype(jnp.float32)
    y = xf * lax.rsqrt(jnp.mean(xf * xf, axis=-1, keepdims=True) + RMS_EPS)
    return (y * g.astype(jnp.float32)).astype(x.dtype)


def causal_dwconv(u, w):
    width, ch = w.shape
    return lax.conv_general_dilated(
        u, w.astype(u.dtype)[:, None, :], window_strides=(1,),
        padding=[(width - 1, 0)], dimension_numbers=('NWC', 'WIO', 'NWC'),
        feature_group_count=ch)


def pool_mixer(u, pool_w, pool_scale):
    S = u.shape[1]
    count = jnp.arange(1, S + 1, dtype=jnp.float32)[None, :, None]
    outs = []
    for g, w in enumerate(POOL_WINDOWS):
        ug = u[..., g * POOL_GROUP_DIM:(g + 1) * POOL_GROUP_DIM]
        cs = jnp.cumsum(ug.astype(jnp.float32), axis=1)
        cs_prev = jnp.pad(cs, ((0, 0), (w, 0), (0, 0)))[:, :S]
        mean = (cs - cs_prev) / jnp.minimum(count, float(w))
        outs.append((mean.astype(u.dtype) - ug) @ pool_w[g])
    return jnp.concatenate(outs, axis=-1) * pool_scale


def ssd_chunked(xh, dt, a, bm, cm):
    Bsz, S, H, P = xh.shape
    G, N = bm.shape[2], bm.shape[3]
    R = H // G
    nc = S // SSM_CHUNK
    L = SSM_CHUNK
    x = xh.reshape(Bsz, nc, L, G, R, P).astype(jnp.float32)
    dtc = dt.reshape(Bsz, nc, L, G, R)
    b = bm.reshape(Bsz, nc, L, G, N).astype(jnp.float32)
    c = cm.reshape(Bsz, nc, L, G, N).astype(jnp.float32)
    dA = dtc * a.reshape(G, R)
    cs = jnp.moveaxis(jnp.cumsum(dA, axis=2), 2, -1)
    xdt = x * dtc[..., None]
    causal = jnp.tril(jnp.ones((L, L), dtype=bool))
    diff = cs[..., :, None] - cs[..., None, :]
    decay = jnp.exp(jnp.where(causal, diff, -jnp.inf))
    cb = jnp.einsum('bclgn,bcsgn->bcgls', c, b)
    y_diag = jnp.einsum('bcgls,bcgrls,bcsgrp->bclgrp', cb, decay, xdt)
    decay_to_end = jnp.exp(cs[..., -1:] - cs)
    states = jnp.einsum('bclgn,bcgrl,bclgrp->bcgrpn', b, decay_to_end, xdt)
    chunk_decay = jnp.exp(cs[..., -1])

    def step(h, inp):
        st, dec = inp
        return dec[..., None, None] * h + st, h

    h0 = jnp.zeros((Bsz, G, R, P, N), jnp.float32)
    _, prev = lax.scan(step, h0, (jnp.moveaxis(states, 1, 0), jnp.moveaxis(chunk_decay, 1, 0)))
    prev = jnp.moveaxis(prev, 0, 1)
    y_off = jnp.einsum('bclgn,bcgrpn,bcgrl->bclgrp', c, prev, jnp.exp(cs))
    return (y_diag + y_off).reshape(Bsz, S, H, P)


def mixer_ab(h, w_in, pool_w, pool_scale, conv_w, conv_b, dt_bias, a_log, d_skip, norm_w, w_out):
    Bsz, S, _ = h.shape
    u = h @ w_in
    o1 = POOL_WIDTH
    o2 = o1 + SSM_INNER
    o3 = o2 + SSM_CONV_DIM
    pool_u, z, xbc, dt_raw = u[..., :o1], u[..., o1:o2], u[..., o2:o3], u[..., o3:]
    pool_out = pool_mixer(pool_u, pool_w, pool_scale)
    xbc = jax.nn.silu(causal_dwconv(xbc, conv_w) + conv_b)
    xs = xbc[..., :SSM_INNER]
    bm = xbc[..., SSM_INNER:SSM_INNER + SSM_GN].reshape(Bsz, S, SSM_GROUPS, SSM_STATE)
    cm = xbc[..., SSM_INNER + SSM_GN:].reshape(Bsz, S, SSM_GROUPS, SSM_STATE)
    dt = jax.nn.softplus((dt_raw + dt_bias).astype(jnp.float32))
    a = -jnp.exp(a_log.astype(jnp.float32))
    xh = xs.reshape(Bsz, S, SSM_HEADS, SSM_HEAD_DIM)
    y = ssd_chunked(xh, dt, a, bm, cm)
    y = y + d_skip.astype(jnp.float32)[:, None] * xh.astype(jnp.float32)
    gsz = SSM_INNER // SSM_GROUPS
    y = y.reshape(Bsz, S, SSM_GROUPS, gsz) * jax.nn.silu(z.astype(jnp.float32)).reshape(Bsz, S, SSM_GROUPS, gsz)
    y = y * lax.rsqrt(jnp.mean(y * y, axis=-1, keepdims=True) + RMS_EPS)
    y = y.reshape(Bsz, S, SSM_INNER) * norm_w.astype(jnp.float32)
    mix = jnp.concatenate([pool_out, y.astype(h.dtype)], axis=-1)
    return mix @ w_out


def mixer_cd(h, w_in, dw_w, dw_b, ln_g, ln_b, sc_w, w_out):
    u = h @ w_in
    v = u[..., :CONF_DIM] * jax.nn.sigmoid(u[..., CONF_DIM:2 * CONF_DIM])
    v = causal_dwconv(v, dw_w) + dw_b
    vf = v.astype(jnp.float32)
    mu = jnp.mean(vf, axis=-1, keepdims=True)
    var = jnp.mean(jnp.square(vf - mu), axis=-1, keepdims=True)
    vn = (vf - mu) * lax.rsqrt(var + LN_EPS) * ln_g.astype(jnp.float32) + ln_b.astype(jnp.float32)
    conf_out = jax.nn.silu(vn).astype(h.dtype)
    off = 2 * CONF_DIM
    bg = u[..., off:off + SC_DIM]
    cg = u[..., off + SC_DIM:off + 2 * SC_DIM]
    hh = u[..., off + 2 * SC_DIM:]
    sc_out = bg * causal_dwconv(cg * hh, sc_w)
    return jnp.concatenate([conf_out, sc_out], axis=-1) @ w_out


def cross_attn(h, mem, wq, wkv, wo):
    Bsz, S, D = h.shape
    M = mem.shape[1]
    q = (h @ wq).reshape(Bsz, S, XA_HEADS, XA_HEAD_DIM)
    kv = mem @ wkv
    k = kv[..., :D].reshape(Bsz, M, XA_HEADS, XA_HEAD_DIM)
    v = kv[..., D:].reshape(Bsz, M, XA_HEADS, XA_HEAD_DIM)
    s = jnp.einsum('bshd,bmhd->bhsm', q, k).astype(jnp.float32) / math.sqrt(XA_HEAD_DIM)
    p = jax.nn.softmax(s, axis=-1).astype(h.dtype)
    o = jnp.einsum('bhsm,bmhd->bshd', p, v).reshape(Bsz, S, D)
    return o @ wo


def sq_relu_mlp(h, w1, w2):
    r = jax.nn.relu(h @ w1)
    return (r * r) @ w2


def setup_inputs(seed: int = 0) -> dict:
    key = jax.random.key(seed)
    ks = jax.random.split(key, 26)
    f32 = jnp.float32

    def nrm(k, shape, scale):
        return scale * jax.random.normal(k, shape, f32)

    dt0 = jnp.exp(jax.random.uniform(ks[14], (N_EVEN, SSM_HEADS), f32, math.log(1e-3), math.log(1e-1)))
    return {
        'x': nrm(ks[0], (BATCH, SEQ, D_MODEL), 1.0),
        'mem': nrm(ks[1], (BATCH, N_MEM, D_MODEL), 1.0),
        'norm_gains': 1.0 + nrm(ks[2], (DEPTH, 6, D_MODEL), 0.05),
        'xa_wq': nrm(ks[3], (DEPTH, D_MODEL, D_MODEL), D_MODEL ** -0.5),
        'xa_wkv': nrm(ks[4], (DEPTH, D_MODEL, 2 * D_MODEL), D_MODEL ** -0.5),
        'xa_wo': nrm(ks[5], (DEPTH, D_MODEL, D_MODEL), D_MODEL ** -0.5),
        'mlp_w1': nrm(ks[6], (DEPTH, D_MODEL, MLP_HIDDEN), D_MODEL ** -0.5),
        'mlp_w2': nrm(ks[7], (DEPTH, MLP_HIDDEN, D_MODEL), MLP_HIDDEN ** -0.5),
        'ab_w_in': nrm(ks[8], (N_EVEN, D_MODEL, AB_IN), D_MODEL ** -0.5),
        'pool_w': nrm(ks[9], (N_EVEN, len(POOL_WINDOWS), POOL_GROUP_DIM, POOL_GROUP_DIM), POOL_GROUP_DIM ** -0.5),
        'pool_scale': 1.0 + nrm(ks[10], (N_EVEN, POOL_WIDTH), 0.1),
        'ssm_conv_w': nrm(ks[11], (N_EVEN, SSM_CONV, SSM_CONV_DIM), SSM_CONV ** -0.5),
        'ssm_conv_b': nrm(ks[12], (N_EVEN, SSM_CONV_DIM), 0.02),
        'ssm_dt_bias': dt0 + jnp.log(-jnp.expm1(-dt0)),
        'ssm_a_log': jnp.log(jax.random.uniform(ks[15], (N_EVEN, SSM_HEADS), f32, 1.0, 16.0)),
        'ssm_d': 1.0 + nrm(ks[16], (N_EVEN, SSM_HEADS), 0.1),
        'ssm_norm': 1.0 + nrm(ks[17], (N_EVEN, SSM_INNER), 0.05),
        'ab_w_out': nrm(ks[18], (N_EVEN, AB_OUT, D_MODEL), AB_OUT ** -0.5),
        'cd_w_in': nrm(ks[19], (N_ODD, D_MODEL, CD_IN), D_MODEL ** -0.5),
        'conf_dw_w': nrm(ks[20], (N_ODD, CONF_KERNEL, CONF_DIM), CONF_KERNEL ** -0.5),
        'conf_dw_b': nrm(ks[21], (N_ODD, CONF_DIM), 0.02),
        'conf_ln_g': 1.0 + nrm(ks[22], (N_ODD, CONF_DIM), 0.05),
        'conf_ln_b': nrm(ks[23], (N_ODD, CONF_DIM), 0.02),
        'sc_conv_w': nrm(ks[24], (N_ODD, SC_KERNEL, SC_DIM), SC_KERNEL ** -0.5),
        'cd_w_out': nrm(ks[25], (N_ODD, CD_OUT, D_MODEL), CD_OUT ** -0.5),
    }


def reference(x, mem, norm_gains, xa_wq, xa_wkv, xa_wo, mlp_w1, mlp_w2,
              ab_w_in, pool_w, pool_scale, ssm_conv_w, ssm_conv_b, ssm_dt_bias,
              ssm_a_log, ssm_d, ssm_norm, ab_w_out,
              cd_w_in, conf_dw_w, conf_dw_b, conf_ln_g, conf_ln_b, sc_conv_w, cd_w_out):
    for layer in range(DEPTH):
        g = norm_gains[layer]
        i = layer // 2
        h = rmsnorm(x, g[0])
        if layer % 2 == 0:
            m = mixer_ab(h, ab_w_in[i], pool_w[i], pool_scale[i], ssm_conv_w[i], ssm_conv_b[i],
                         ssm_dt_bias[i], ssm_a_log[i], ssm_d[i], ssm_norm[i], ab_w_out[i])
        else:
            m = mixer_cd(h, cd_w_in[i], conf_dw_w[i], conf_dw_b[i], conf_ln_g[i], conf_ln_b[i],
                         sc_conv_w[i], cd_w_out[i])
        x = x + rmsnorm(m, g[1])
        h = rmsnorm(x, g[2])
        x = x + rmsnorm(cross_attn(h, mem, xa_wq[layer], xa_wkv[layer], xa_wo[layer]), g[3])
        h = rmsnorm(x, g[4])
        x = x + rmsnorm(sq_relu_mlp(h, mlp_w1[layer], mlp_w2[layer]), g[5])
    return x
```

```python
import functools

import jax
import jax.numpy as jnp
from jax import lax
from jax.experimental import pallas as pl
from jax.experimental.pallas import tpu as pltpu

F32 = jnp.float32
BF16 = jnp.bfloat16

D_MODEL = 1024
N_MEM = 256
XA_HEADS = 4
XA_HEAD_DIM = D_MODEL // XA_HEADS

POOL_WINDOWS = (2, 4, 8, 16)
POOL_GROUP_DIM = 128
POOL_WIDTH = len(POOL_WINDOWS) * POOL_GROUP_DIM
POOL_HALO = 16

SSM_HEAD_DIM = 64
SSM_INNER = 1024
SSM_HEADS = 16
SSM_GROUPS = 2
SSM_STATE = 128
SSM_CONV = 4
SSM_CHUNK = 128
SSM_GN = SSM_GROUPS * SSM_STATE
SSM_CONV_DIM = SSM_INNER + 2 * SSM_GN
AB_IN = POOL_WIDTH + SSM_INNER + SSM_CONV_DIM + SSM_HEADS
AB_OUT = POOL_WIDTH + SSM_INNER
LANES = 128
AB_IN_PAD = POOL_WIDTH + SSM_INNER + SSM_CONV_DIM + LANES
OFF_Z = POOL_WIDTH
OFF_XBC = OFF_Z + SSM_INNER
OFF_DT = OFF_XBC + SSM_CONV_DIM

CONF_DIM = 1024
CONF_KERNEL = 31
CONF_HALO = 32
SC_DIM = 1024
SC_KERNEL = 3
SC_HALO = 8
CD_IN = 2 * CONF_DIM + 3 * SC_DIM
CD_OUT = CONF_DIM + SC_DIM

MLP_HIDDEN = 4096
MLP_CHUNK = 512
RMS_EPS = 1e-6
LN_EPS = 1e-5

SEQ_TILE = 512
VMEM_LIMIT = 56 * 1024 * 1024


def _rms(x, g):
    return x * lax.rsqrt(jnp.mean(x * x, axis=-1, keepdims=True) + RMS_EPS) * g


def _silu(x):
    return x * jax.nn.sigmoid(x)


def _dot(a, b):
    return jnp.dot(a, b, preferred_element_type=F32)


def _const_spec(shape):
    nd = len(shape)
    return pl.BlockSpec(shape, lambda *_: (0,) * nd, pipeline_mode=pl.Buffered(1))


def _ab_kernel(x_ref, g_ref, win_ref, poolw_ref, pscale_ref, convw_ref, convb_ref, dtb_ref,
               alog_ref, dskip_ref, normw_ref, o_ref, u_s, xbc_s, state_s):
    tq = x_ref.shape[0]
    s_idx = pl.program_id(1)
    H0 = POOL_HALO

    @pl.when(s_idx == 0)
    def _():
        u_s[0:H0, :] = jnp.zeros((H0, AB_IN_PAD), F32)
        state_s[...] = jnp.zeros_like(state_s)

    h = _rms(x_ref[...], g_ref[0:1, :]).astype(BF16)
    u_s[H0:H0 + tq, :] = _dot(h, win_ref[...])

    pos = (s_idx * tq + 1 + lax.broadcasted_iota(jnp.int32, (tq, 1), 0)).astype(F32)
    for g, w in enumerate(POOL_WINDOWS):
        c0 = g * POOL_GROUP_DIM
        cur = u_s[H0:H0 + tq, c0:c0 + POOL_GROUP_DIM]
        acc = cur
        for j in range(1, w):
            acc = acc + u_s[H0 - j:H0 - j + tq, c0:c0 + POOL_GROUP_DIM]
        mean = acc / jnp.minimum(pos, float(w))
        po = _dot((mean - cur).astype(BF16), poolw_ref[g]) * pscale_ref[:, c0:c0 + POOL_GROUP_DIM]
        o_ref[:, c0:c0 + POOL_GROUP_DIM] = po.astype(o_ref.dtype)

    acc = convb_ref[...]
    for k in range(SSM_CONV):
        r0 = H0 - (SSM_CONV - 1) + k
        acc = acc + convw_ref[k:k + 1, :] * u_s[r0:r0 + tq, OFF_XBC:OFF_XBC + SSM_CONV_DIM]
    xbc_s[...] = _silu(acc)

    a_row = -jnp.exp(alog_ref[...])
    L = SSM_CHUNK
    ri = lax.broadcasted_iota(jnp.int32, (L, L), 0)
    ci = lax.broadcasted_iota(jnp.int32, (L, L), 1)
    causal = ri >= ci
    tril = jnp.where(causal, 1.0, 0.0).astype(BF16)
    lo_half = lax.broadcasted_iota(jnp.int32, (1, LANES), 1) < SSM_HEAD_DIM
    heads_per_group = SSM_HEADS // SSM_GROUPS
    slabs_per_group = heads_per_group // 2
    group_width = SSM_INNER // SSM_GROUPS

    def chunk(c, carry):
        r0 = pl.multiple_of(c * L, L)
        rows = pl.ds(r0, L)
        urows = pl.ds(H0 + r0, L)
        dtp = jax.nn.softplus(u_s[urows, OFF_DT:OFF_DT + LANES] + dtb_ref[...])
        d_a = dtp * a_row
        hi = d_a.astype(BF16)
        r1 = d_a - hi.astype(F32)
        mid = r1.astype(BF16)
        lo = (r1 - mid.astype(F32)).astype(BF16)
        cs = _dot(tril, hi) + _dot(tril, mid) + _dot(tril, lo)
        cs_t = cs.T

        for g in range(SSM_GROUPS):
            bm = xbc_s[rows, SSM_INNER + g * SSM_STATE:SSM_INNER + (g + 1) * SSM_STATE]
            cm = xbc_s[rows, SSM_INNER + SSM_GN + g * SSM_STATE:SSM_INNER + SSM_GN + (g + 1) * SSM_STATE]
            bm_b = bm.astype(BF16)
            cm_b = cm.astype(BF16)
            bm_t = bm.T.astype(BF16)
            cb = lax.dot_general(cm_b, bm_b, (((1,), (1,)), ((), ())), preferred_element_type=F32)
            ys = []
            ssq = jnp.zeros((L, 1), F32)
            for sl in range(slabs_per_group):
                slab = g * slabs_per_group + sl
                h0 = 2 * slab
                cols = slice(slab * LANES, (slab + 1) * LANES)
                col0 = jnp.broadcast_to(cs[:, h0:h0 + 1], (L, L))
                col1 = jnp.broadcast_to(cs[:, h0 + 1:h0 + 2], (L, L))
                row0 = jnp.broadcast_to(cs_t[h0:h0 + 1, :], (L, L))
                row1 = jnp.broadcast_to(cs_t[h0 + 1:h0 + 2, :], (L, L))
                dt_pair = jnp.where(lo_half, jnp.broadcast_to(dtp[:, h0:h0 + 1], (L, LANES)),
                                    jnp.broadcast_to(dtp[:, h0 + 1:h0 + 2], (L, LANES)))
                cs_pair = jnp.where(lo_half, col0, col1)
                xs = xbc_s[rows, cols]
                xdt = xs * dt_pair
                xdt_b = xdt.astype(BF16)
                zero_b = jnp.zeros_like(xdt_b)
                m0 = (cb * jnp.exp(jnp.where(causal, col0 - row0, -jnp.inf))).astype(BF16)
                m1 = (cb * jnp.exp(jnp.where(causal, col1 - row1, -jnp.inf))).astype(BF16)
                y = _dot(m0, jnp.where(lo_half, xdt_b, zero_b)) + _dot(m1, jnp.where(lo_half, zero_b, xdt_b))
                hs = state_s[:, cols]
                y = y + _dot(cm_b, hs.astype(BF16)) * jnp.exp(cs_pair)
                last = cs_pair[L - 1:L, :]
                xw = (xdt * jnp.exp(last - cs_pair)).astype(BF16)
                state_s[:, cols] = hs * jnp.exp(last) + _dot(bm_t, xw)
                y = y + dskip_ref[:, cols] * xs
                y = y * _silu(u_s[urows, OFF_Z + slab * LANES:OFF_Z + (slab + 1) * LANES])
                ssq = ssq + jnp.sum(y * y, axis=-1, keepdims=True)
                ys.append(y)
            scale = lax.rsqrt(ssq * (1.0 / group_width) + RMS_EPS)
            for sl in range(slabs_per_group):
                slab = g * slabs_per_group + sl
                cols = slice(slab * LANES, (slab + 1) * LANES)
                yo = ys[sl] * scale * normw_ref[:, cols]
                o_ref[rows, POOL_WIDTH + slab * LANES:POOL_WIDTH + (slab + 1) * LANES] = yo.astype(o_ref.dtype)
        return carry

    lax.fori_loop(0, tq // L, chunk, 0)

    u_s[0:H0, :] = u_s[tq:tq + H0, :]


def _mixer_ab(x, gains, w_in, pool_w, pool_scale, conv_w, conv_b, dt_bias, a_log, d_skip, norm_w):
    B, S, D = x.shape
    tq = min(SEQ_TILE, S)
    pad = AB_IN_PAD - AB_IN
    win = jnp.pad(w_in, ((0, 0), (0, pad))).astype(BF16)
    row = lambda v: v.reshape(1, -1).astype(F32)
    lane_pad = lambda v: jnp.pad(v.astype(F32), (0, LANES - v.shape[0])).reshape(1, LANES)
    args = (
        gains, win, pool_w.astype(BF16), row(pool_scale), conv_w.astype(F32), row(conv_b),
        lane_pad(dt_bias), lane_pad(a_log), row(jnp.repeat(d_skip, SSM_HEAD_DIM)), row(norm_w),
    )
    return pl.pallas_call(
        _ab_kernel,
        grid=(B, S // tq),
        in_specs=[pl.BlockSpec((None, tq, D), lambda b, s: (b, s, 0))] + [_const_spec(a.shape) for a in args],
        out_specs=pl.BlockSpec((None, tq, AB_OUT), lambda b, s: (b, s, 0)),
        out_shape=jax.ShapeDtypeStruct((B, S, AB_OUT), BF16),
        scratch_shapes=[
            pltpu.VMEM((POOL_HALO + tq, AB_IN_PAD), F32),
            pltpu.VMEM((tq, SSM_CONV_DIM), F32),
            pltpu.VMEM((SSM_STATE, SSM_INNER), F32),
        ],
        compiler_params=pltpu.CompilerParams(
            dimension_semantics=("arbitrary", "arbitrary"), vmem_limit_bytes=VMEM_LIMIT),
        name="mixer_ab",
    )(x, *args)


def _cd_kernel(x_ref, g_ref, win_ref, dww_ref, dwb_ref, lng_ref, lnb_ref, scw_ref, o_ref,
               u_s, v_s, p_s):
    tq = x_ref.shape[0]
    s_idx = pl.program_id(1)
    HV, HP = CONF_HALO, SC_HALO

    @pl.when(s_idx == 0)
    def _():
        v_s[0:HV, :] = jnp.zeros((HV, CONF_DIM), F32)
        p_s[0:HP, :] = jnp.zeros((HP, SC_DIM), F32)

    h = _rms(x_ref[...], g_ref[0:1, :]).astype(BF16)
    u_s[...] = _dot(h, win_ref[...])

    v_s[HV:HV + tq, :] = u_s[:, 0:CONF_DIM] * jax.nn.sigmoid(u_s[:, CONF_DIM:2 * CONF_DIM])
    acc = dwb_ref[...]
    for k in range(CONF_KERNEL):
        r0 = HV - (CONF_KERNEL - 1) + k
        acc = acc + dww_ref[k:k + 1, :] * v_s[r0:r0 + tq, :]
    mu = jnp.mean(acc, axis=-1, keepdims=True)
    cen = acc - mu
    var = jnp.mean(cen * cen, axis=-1, keepdims=True)
    vn = cen * lax.rsqrt(var + LN_EPS) * lng_ref[...] + lnb_ref[...]
    o_ref[:, 0:CONF_DIM] = _silu(vn).astype(o_ref.dtype)
    v_s[0:HV, :] = v_s[tq:tq + HV, :]

    off = 2 * CONF_DIM
    p_s[HP:HP + tq, :] = u_s[:, off + SC_DIM:off + 2 * SC_DIM] * u_s[:, off + 2 * SC_DIM:off + 3 * SC_DIM]
    acc = jnp.zeros((tq, SC_DIM), F32)
    for k in range(SC_KERNEL):
        r0 = HP - (SC_KERNEL - 1) + k
        acc = acc + scw_ref[k:k + 1, :] * p_s[r0:r0 + tq, :]
    o_ref[:, CONF_DIM:CONF_DIM + SC_DIM] = (u_s[:, off:off + SC_DIM] * acc).astype(o_ref.dtype)
    p_s[0:HP, :] = p_s[tq:tq + HP, :]


def _mixer_cd(x, gains, w_in, dw_w, dw_b, ln_g, ln_b, sc_w):
    B, S, D = x.shape
    tq = min(SEQ_TILE, S)
    row = lambda v: v.reshape(1, -1).astype(F32)
    args = (gains, w_in.astype(BF16), dw_w.astype(F32), row(dw_b), row(ln_g), row(ln_b), sc_w.astype(F32))
    return pl.pallas_call(
        _cd_kernel,
        grid=(B, S // tq),
        in_specs=[pl.BlockSpec((None, tq, D), lambda b, s: (b, s, 0))] + [_const_spec(a.shape) for a in args],
        out_specs=pl.BlockSpec((None, tq, CD_OUT), lambda b, s: (b, s, 0)),
        out_shape=jax.ShapeDtypeStruct((B, S, CD_OUT), BF16),
        scratch_shapes=[
            pltpu.VMEM((tq, CD_IN), F32),
            pltpu.VMEM((CONF_HALO + tq, CONF_DIM), F32),
            pltpu.VMEM((SC_HALO + tq, SC_DIM), F32),
        ],
        compiler_params=pltpu.CompilerParams(
            dimension_semantics=("arbitrary", "arbitrary"), vmem_limit_bytes=VMEM_LIMIT),
        name="mixer_cd",
    )(x, *args)


def _kv_kernel(mem_ref, wkv_ref, o_ref):
    o_ref[...] = _dot(mem_ref[...].astype(BF16), wkv_ref[...]).astype(o_ref.dtype)


def _kv_proj(mem, wkv):
    B, M, D = mem.shape
    return pl.pallas_call(
        _kv_kernel,
        grid=(B,),
        in_specs=[pl.BlockSpec((None, M, D), lambda b: (b, 0, 0)), _const_spec(wkv.shape)],
        out_specs=pl.BlockSpec((None, M, 2 * D), lambda b: (b, 0, 0)),
        out_shape=jax.ShapeDtypeStruct((B, M, 2 * D), BF16),
        compiler_params=pltpu.CompilerParams(
            dimension_semantics=("arbitrary",), vmem_limit_bytes=VMEM_LIMIT),
        name="kv_proj",
    )(mem, wkv.astype(BF16))


def _post_kernel(x_ref, mix_ref, kv_ref, g_ref, wout_ref, wq_ref, wo_ref, w1_ref, w2_ref, o_ref):
    tm = x_ref.shape[0]
    x = x_ref[...]

    x = x + _rms(_dot(mix_ref[...], wout_ref[...]), g_ref[1:2, :])

    q = _dot(_rms(x, g_ref[2:3, :]).astype(BF16), wq_ref[...])
    heads = []
    for hd in range(XA_HEADS):
        cols = slice(hd * XA_HEAD_DIM, (hd + 1) * XA_HEAD_DIM)
        kh = kv_ref[:, cols]
        vh = kv_ref[:, D_MODEL + hd * XA_HEAD_DIM:D_MODEL + (hd + 1) * XA_HEAD_DIM]
        s = lax.dot_general(q[:, cols].astype(BF16), kh, (((1,), (1,)), ((), ())),
                            preferred_element_type=F32) * (1.0 / XA_HEAD_DIM ** 0.5)
        e = jnp.exp(s - jnp.max(s, axis=-1, keepdims=True))
        p = e * (1.0 / jnp.sum(e, axis=-1, keepdims=True))
        heads.append(_dot(p.astype(BF16), vh).astype(BF16))
    o = jnp.concatenate(heads, axis=-1)
    x = x + _rms(_dot(o, wo_ref[...]), g_ref[3:4, :])

    h = _rms(x, g_ref[4:5, :]).astype(BF16)
    acc = jnp.zeros((tm, D_MODEL), F32)
    for c in range(0, MLP_HIDDEN, MLP_CHUNK):
        r = jnp.maximum(_dot(h, w1_ref[:, c:c + MLP_CHUNK]), 0.0)
        acc = acc + _dot((r * r).astype(BF16), w2_ref[c:c + MLP_CHUNK, :])
    o_ref[...] = x + _rms(acc, g_ref[5:6, :])


def _post(x, mix, kv, gains, w_out, wq, wo, w1, w2):
    B, S, D = x.shape
    tm = min(SEQ_TILE, S)
    ws = tuple(w.astype(BF16) for w in (w_out, wq, wo, w1, w2))
    return pl.pallas_call(
        _post_kernel,
        grid=(B, S // tm),
        in_specs=[
            pl.BlockSpec((None, tm, D), lambda b, s: (b, s, 0)),
            pl.BlockSpec((None, tm, mix.shape[-1]), lambda b, s: (b, s, 0)),
            pl.BlockSpec((None, kv.shape[1], kv.shape[2]), lambda b, s: (b, 0, 0)),
            _const_spec(gains.shape),
        ] + [_const_spec(w.shape) for w in ws],
        out_specs=pl.BlockSpec((None, tm, D), lambda b, s: (b, s, 0)),
        out_shape=jax.ShapeDtypeStruct((B, S, D), F32),
        compiler_params=pltpu.CompilerParams(
            dimension_semantics=("arbitrary", "arbitrary"), vmem_limit_bytes=VMEM_LIMIT),
        name="post",
    )(x, mix, kv, gains, *ws)


def kernel(x, mem, norm_gains, xa_wq, xa_wkv, xa_wo, mlp_w1, mlp_w2, ab_w_in, pool_w, pool_scale, ssm_conv_w, ssm_conv_b, ssm_dt_bias, ssm_a_log, ssm_d, ssm_norm, ab_w_out, cd_w_in, conf_dw_w, conf_dw_b, conf_ln_g, conf_ln_b, sc_conv_w, cd_w_out):
    depth = norm_gains.shape[0]
    for layer in range(depth):
        g = norm_gains[layer].astype(F32)
        i = layer // 2
        if layer % 2 == 0:
            mix = _mixer_ab(x, g, ab_w_in[i], pool_w[i], pool_scale[i], ssm_conv_w[i], ssm_conv_b[i],
                            ssm_dt_bias[i], ssm_a_log[i], ssm_d[i], ssm_norm[i])
            w_out = ab_w_out[i]
        else:
            mix = _mixer_cd(x, g, cd_w_in[i], conf_dw_w[i], conf_dw_b[i], conf_ln_g[i], conf_ln_b[i],
                            sc_conv_w[i])
            w_out = cd_w_out[i]
        kv = _kv_proj(mem, xa_wkv[layer])
        x = _post(x, mix, kv, g, w_out, xa_wq[layer], xa_wo[layer], mlp_w1[layer], mlp_w2[layer])
    return x
```

```python
import jax
import jax.numpy as jnp
from jax import lax
from jax.experimental import pallas as pl
from jax.experimental.pallas import tpu as pltpu

F32 = jnp.float32
BF16 = jnp.bfloat16

D_MODEL = 1024
N_MEM = 256
XA_HEADS = 4
XA_HEAD_DIM = D_MODEL // XA_HEADS

LANES = 128
MXU_N = 256

POOL_WINDOWS = (2, 4, 8, 16)
POOL_GROUP_DIM = 128
POOL_WIDTH = len(POOL_WINDOWS) * POOL_GROUP_DIM
POOL_HALO = 16

SSM_HEAD_DIM = 64
SSM_INNER = 1024
SSM_HEADS = 16
SSM_GROUPS = 2
SSM_STATE = 128
SSM_CONV = 4
SSM_CONV_HALO = 8
SSM_CHUNK = 128
SSM_GN = SSM_GROUPS * SSM_STATE
SSM_CONV_DIM = SSM_INNER + 2 * SSM_GN
AB_IN = POOL_WIDTH + SSM_INNER + SSM_CONV_DIM + SSM_HEADS
AB_OUT = POOL_WIDTH + SSM_INNER
AB_IN_PAD = POOL_WIDTH + SSM_INNER + SSM_CONV_DIM + LANES
OFF_Z = POOL_WIDTH
OFF_XBC = OFF_Z + SSM_INNER
OFF_DT = OFF_XBC + SSM_CONV_DIM

CONF_DIM = 1024
CONF_KERNEL = 31
CONF_HALO = 32
SC_DIM = 1024
SC_KERNEL = 3
SC_HALO = 8
CD_IN = 2 * CONF_DIM + 3 * SC_DIM
CD_OUT = CONF_DIM + SC_DIM
CONV_ROWS = 64
CD_BLOCKS_PER_GROUP = CD_IN // CONF_DIM

MLP_HIDDEN = 4096
MLP_CHUNK = 512
RMS_EPS = 1e-6
LN_EPS = 1e-5

SEQ_TILE = 512
VMEM_LIMIT = 56 * 1024 * 1024


def _rms(x, g):
    return x * lax.rsqrt(jnp.mean(x * x, axis=-1, keepdims=True) + RMS_EPS) * g


def _silu(x):
    return x * jax.nn.sigmoid(x)


def _dot(a, b):
    return jnp.dot(a, b, preferred_element_type=F32)


def _const_spec(shape):
    nd = len(shape)
    return pl.BlockSpec(shape, lambda *_: (0,) * nd, pipeline_mode=pl.Buffered(1))


def _store_slabs(slab_ref, first_slab, row0, val):
    rows = val.shape[0]
    for j in range(val.shape[1] // LANES):
        slab_ref[first_slab + j, row0:row0 + rows, :] = val[:, j * LANES:(j + 1) * LANES]


def _dwconv(slab_ref, slab, w_ref, cols, width, halo, row0, rows, init):
    acc = init
    for k in range(width):
        r = halo - (width - 1) + k + row0
        acc = acc + w_ref[k:k + 1, cols] * slab_ref[slab, r:r + rows, :]
    return acc


def _ab_kernel(x_ref, g_ref, win_ref, poolw_ref, pscale_ref, convw_ref, convb_ref, dtb_ref,
               alog_ref, dskip_ref, normw_ref, o_ref, pool_s, xc_s, y_s, state_s):
    tq = x_ref.shape[0]
    s_idx = pl.program_id(1)
    HP, HC, L = POOL_HALO, SSM_CONV_HALO, SSM_CHUNK
    n_chunks = tq // L
    x_slabs = SSM_INNER // LANES
    b_slab0 = x_slabs
    c_slab0 = x_slabs + SSM_GN // LANES

    @pl.when(s_idx == 0)
    def _():
        pool_s[:, 0:HP, :] = jnp.zeros((pool_s.shape[0], HP, LANES), F32)
        xc_s[:, 0:HC, :] = jnp.zeros((xc_s.shape[0], HC, LANES), F32)
        state_s[...] = jnp.zeros_like(state_s)

    h = _rms(x_ref[...], g_ref[0:1, :]).astype(BF16)

    def conv_silu(slab, row0):
        cols = slice(slab * LANES, (slab + 1) * LANES)
        return _silu(_dwconv(xc_s, slab, convw_ref, cols, SSM_CONV, HC, row0, L, convb_ref[:, cols]))

    _store_slabs(pool_s, 0, HP, _dot(h, win_ref[:, 0:POOL_WIDTH]))
    pos = (s_idx * tq + 1 + lax.broadcasted_iota(jnp.int32, (tq, 1), 0)).astype(F32)
    for g, w in enumerate(POOL_WINDOWS):
        c0 = g * POOL_GROUP_DIM
        cur = pool_s[g, HP:HP + tq, :]
        acc = cur
        for j in range(1, w):
            acc = acc + pool_s[g, HP - j:HP - j + tq, :]
        mean = acc / jnp.minimum(pos, float(w))
        po = _dot((mean - cur).astype(BF16), poolw_ref[g]) * pscale_ref[:, c0:c0 + POOL_GROUP_DIM]
        o_ref[:, c0:c0 + POOL_GROUP_DIM] = po.astype(o_ref.dtype)

    a_row = -jnp.exp(alog_ref[...])
    ri = lax.broadcasted_iota(jnp.int32, (L, L), 0)
    ci = lax.broadcasted_iota(jnp.int32, (L, L), 1)
    causal = ri >= ci
    tril = jnp.where(causal, 1.0, 0.0).astype(BF16)
    lo_half = lax.broadcasted_iota(jnp.int32, (1, LANES), 1) < SSM_HEAD_DIM
    dt_raw = _dot(h, win_ref[:, OFF_DT:OFF_DT + LANES])
    dtps, css, cs_ts = [], [], []
    for c in range(n_chunks):
        dtp = jax.nn.softplus(dt_raw[c * L:(c + 1) * L, :] + dtb_ref[...])
        d_a = dtp * a_row
        hi = d_a.astype(BF16)
        r1 = d_a - hi.astype(F32)
        mid = r1.astype(BF16)
        lo = (r1 - mid.astype(F32)).astype(BF16)
        cs = _dot(tril, hi) + _dot(tril, mid) + _dot(tril, lo)
        dtps.append(dtp)
        css.append(cs)
        cs_ts.append(cs.T)

    _store_slabs(xc_s, b_slab0, HC, _dot(h, win_ref[:, OFF_XBC + SSM_INNER:OFF_XBC + SSM_CONV_DIM]))
    cm_bs, bm_ts, cbs = [], [], []
    for c in range(n_chunks):
        cm_c, bt_c, cb_c = [], [], []
        for g in range(SSM_GROUPS):
            bm = conv_silu(b_slab0 + g, c * L)
            cm_b = conv_silu(c_slab0 + g, c * L).astype(BF16)
            cm_c.append(cm_b)
            bt_c.append(bm.T.astype(BF16))
            cb_c.append(lax.dot_general(cm_b, bm.astype(BF16), (((1,), (1,)), ((), ())),
                                        preferred_element_type=F32))
        cm_bs.append(cm_c)
        bm_ts.append(bt_c)
        cbs.append(cb_c)

    slabs_per_group = x_slabs // SSM_GROUPS
    group_width = SSM_INNER // SSM_GROUPS
    ssq = [[jnp.zeros((L, 1), F32) for _ in range(n_chunks)] for _ in range(SSM_GROUPS)]
    for sp in range(SSM_INNER // MXU_N):
        c0 = sp * MXU_N
        _store_slabs(xc_s, 2 * sp, HC, _dot(h, win_ref[:, OFF_XBC + c0:OFF_XBC + c0 + MXU_N]))
        uz = _dot(h, win_ref[:, OFF_Z + c0:OFF_Z + c0 + MXU_N])
        for j in range(MXU_N // LANES):
            slab = 2 * sp + j
            g = slab // slabs_per_group
            h0 = 2 * slab
            cols = slice(slab * LANES, (slab + 1) * LANES)
            for c in range(n_chunks):
                rows = slice(c * L, (c + 1) * L)
                cs, cs_t, dtp = css[c], cs_ts[c], dtps[c]
                xs = conv_silu(slab, c * L)
                col0 = jnp.broadcast_to(cs[:, h0:h0 + 1], (L, L))
                col1 = jnp.broadcast_to(cs[:, h0 + 1:h0 + 2], (L, L))
                row0 = jnp.broadcast_to(cs_t[h0:h0 + 1, :], (L, L))
                row1 = jnp.broadcast_to(cs_t[h0 + 1:h0 + 2, :], (L, L))
                dt_pair = jnp.where(lo_half, jnp.broadcast_to(dtp[:, h0:h0 + 1], (L, LANES)),
                                    jnp.broadcast_to(dtp[:, h0 + 1:h0 + 2], (L, LANES)))
                cs_pair = jnp.where(lo_half, col0, col1)
                xdt = xs * dt_pair
                xdt_b = xdt.astype(BF16)
                zero_b = jnp.zeros_like(xdt_b)
                cb = cbs[c][g]
                m0 = (cb * jnp.exp(jnp.where(causal, col0 - row0, -jnp.inf))).astype(BF16)
                m1 = (cb * jnp.exp(jnp.where(causal, col1 - row1, -jnp.inf))).astype(BF16)
                y = _dot(m0, jnp.where(lo_half, xdt_b, zero_b)) + _dot(m1, jnp.where(lo_half, zero_b, xdt_b))
                hs = state_s[:, cols]
                y = y + _dot(cm_bs[c][g], hs.astype(BF16)) * jnp.exp(cs_pair)
                last = cs_pair[L - 1:L, :]
                xw = (xdt * jnp.exp(last - cs_pair)).astype(BF16)
                state_s[:, cols] = hs * jnp.exp(last) + _dot(bm_ts[c][g], xw)
                y = y + dskip_ref[:, cols] * xs
                y = y * _silu(uz[rows, j * LANES:(j + 1) * LANES])
                ssq[g][c] = ssq[g][c] + jnp.sum(y * y, axis=-1, keepdims=True)
                y_s[rows, cols] = y

    for g in range(SSM_GROUPS):
        gcols = slice(g * group_width, (g + 1) * group_width)
        for c in range(n_chunks):
            rows = slice(c * L, (c + 1) * L)
            scale = lax.rsqrt(ssq[g][c] * (1.0 / group_width) + RMS_EPS)
            yo = y_s[rows, gcols] * scale * normw_ref[:, gcols]
            o_ref[rows, POOL_WIDTH + g * group_width:POOL_WIDTH + (g + 1) * group_width] = yo.astype(o_ref.dtype)

    pool_s[:, 0:HP, :] = pool_s[:, tq:tq + HP, :]
    xc_s[:, 0:HC, :] = xc_s[:, tq:tq + HC, :]


def _mixer_ab(x, gains, w_in, pool_w, pool_scale, conv_w, conv_b, dt_bias, a_log, d_skip, norm_w):
    B, S, D = x.shape
    tq = min(SEQ_TILE, S)
    pad = AB_IN_PAD - AB_IN
    win = jnp.pad(w_in, ((0, 0), (0, pad))).astype(BF16)
    row = lambda v: v.reshape(1, -1).astype(F32)
    lane_pad = lambda v: jnp.pad(v.astype(F32), (0, LANES - v.shape[0])).reshape(1, LANES)
    args = (
        gains, win, pool_w.astype(BF16), row(pool_scale), conv_w.astype(F32), row(conv_b),
        lane_pad(dt_bias), lane_pad(a_log), row(jnp.repeat(d_skip, SSM_HEAD_DIM)), row(norm_w),
    )
    return pl.pallas_call(
        _ab_kernel,
        grid=(B, S // tq),
        in_specs=[pl.BlockSpec((None, tq, D), lambda b, s: (b, s, 0))] + [_const_spec(a.shape) for a in args],
        out_specs=pl.BlockSpec((None, tq, AB_OUT), lambda b, s: (b, s, 0)),
        out_shape=jax.ShapeDtypeStruct((B, S, AB_OUT), BF16),
        scratch_shapes=[
            pltpu.VMEM((POOL_WIDTH // LANES, POOL_HALO + tq, LANES), F32),
            pltpu.VMEM((SSM_CONV_DIM // LANES, SSM_CONV_HALO + tq, LANES), F32),
            pltpu.VMEM((tq, SSM_INNER), F32),
            pltpu.VMEM((SSM_STATE, SSM_INNER), F32),
        ],
        compiler_params=pltpu.CompilerParams(
            dimension_semantics=("arbitrary", "arbitrary"), vmem_limit_bytes=VMEM_LIMIT),
        name="mixer_ab",
    )(x, *args)


def _cd_kernel(x_ref, g_ref, win_ref, dww_ref, dwb_ref, lng_ref, lnb_ref, scw_ref, o_ref, *scratch):
    tq = x_ref.shape[0]
    s_idx = pl.program_id(1)
    HV, HP = CONF_HALO, SC_HALO
    slabs_per_col_group = MXU_N // LANES
    n_groups = CONF_DIM // MXU_N
    v_grp = scratch[0:n_groups]
    p_grp = scratch[n_groups:2 * n_groups]
    c_s, h_s = scratch[2 * n_groups:]

    @pl.when(s_idx == 0)
    def _():
        for v_s in v_grp:
            v_s[:, 0:HV, :] = jnp.zeros((slabs_per_col_group, HV, LANES), F32)
        for p_s in p_grp:
            p_s[:, 0:HP, :] = jnp.zeros((slabs_per_col_group, HP, LANES), F32)


    def project(cg, blk):
        base = (cg * CD_BLOCKS_PER_GROUP + blk) * MXU_N
        return _dot(h_s[...], win_ref[:, base:base + MXU_N])

    def project_glu(cg):
        for j in range(slabs_per_col_group):
            r = project(cg, j)
            v_grp[cg][j, HV:HV + tq, :] = r[:, 0:LANES] * jax.nn.sigmoid(r[:, LANES:])

    def short_conv(cg):
        for j in range(slabs_per_col_group):
            r = project(cg, slabs_per_col_group + j)
            p_grp[cg][j, HP:HP + tq, :] = r[:, 0:LANES] * r[:, LANES:]
        bg = project(cg, 2 * slabs_per_col_group)
        for j in range(slabs_per_col_group):
            slab = cg * slabs_per_col_group + j
            cols = slice(slab * LANES, (slab + 1) * LANES)
            conv = _dwconv(p_grp[cg], j, scw_ref, cols, SC_KERNEL, HP, 0, tq, jnp.zeros((1, LANES), F32))
            o_ref[:, CONF_DIM + slab * LANES:CONF_DIM + (slab + 1) * LANES] = (
                bg[:, j * LANES:(j + 1) * LANES] * conv).astype(o_ref.dtype)

    def conv31_block(cg, blk):
        j, rb = divmod(blk, tq // CONV_ROWS)
        slab = cg * slabs_per_col_group + j
        cols = slice(slab * LANES, (slab + 1) * LANES)
        r0 = rb * CONV_ROWS
        c_s[r0:r0 + CONV_ROWS, cols] = _dwconv(v_grp[cg], j, dww_ref, cols, CONF_KERNEL, HV, r0,
                                               CONV_ROWS, dwb_ref[:, cols])

    blocks_per_group = slabs_per_col_group * (tq // CONV_ROWS)
    third = blocks_per_group // 3

    h_s[...] = _rms(x_ref[...], g_ref[0:1, :]).astype(BF16)
    project_glu(0)
    for cg in range(n_groups):
        for blk in range(0, third):
            conv31_block(cg, blk)
        if cg + 1 < n_groups:
            project_glu(cg + 1)
        for blk in range(third, 2 * third):
            conv31_block(cg, blk)
        short_conv(cg)
        for blk in range(2 * third, blocks_per_group):
            conv31_block(cg, blk)

    for rb in range(tq // CONV_ROWS):
        rows = slice(rb * CONV_ROWS, (rb + 1) * CONV_ROWS)
        acc = c_s[rows, :]
        mu = jnp.mean(acc, axis=-1, keepdims=True)
        cen = acc - mu
        var = jnp.mean(cen * cen, axis=-1, keepdims=True)
        vn = cen * lax.rsqrt(var + LN_EPS) * lng_ref[...] + lnb_ref[...]
        o_ref[rows, 0:CONF_DIM] = _silu(vn).astype(o_ref.dtype)

    for v_s in v_grp:
        v_s[:, 0:HV, :] = v_s[:, tq:tq + HV, :]
    for p_s in p_grp:
        p_s[:, 0:HP, :] = p_s[:, tq:tq + HP, :]


def _cd_weight_layout(w_in):
    val, gate, bg, cgate, hh = range(5)

    def tile(part, s):
        c0 = part * CONF_DIM + s * LANES
        return w_in[:, c0:c0 + LANES]

    cols = []
    for cg in range(CONF_DIM // MXU_N):
        a, b = 2 * cg, 2 * cg + 1
        cols += [tile(val, a), tile(gate, a), tile(val, b), tile(gate, b),
                 tile(cgate, a), tile(hh, a), tile(cgate, b), tile(hh, b), tile(bg, a), tile(bg, b)]
    return jnp.concatenate(cols, axis=1)


def _mixer_cd(x, gains, w_in, dw_w, dw_b, ln_g, ln_b, sc_w):
    B, S, D = x.shape
    tq = min(SEQ_TILE, S)
    row = lambda v: v.reshape(1, -1).astype(F32)
    args = (gains, _cd_weight_layout(w_in).astype(BF16), dw_w.astype(F32), row(dw_b), row(ln_g), row(ln_b),
            sc_w.astype(F32))
    return pl.pallas_call(
        _cd_kernel,
        grid=(B, S // tq),
        in_specs=[pl.BlockSpec((None, tq, D), lambda b, s: (b, s, 0))] + [_const_spec(a.shape) for a in args],
        out_specs=pl.BlockSpec((None, tq, CD_OUT), lambda b, s: (b, s, 0)),
        out_shape=jax.ShapeDtypeStruct((B, S, CD_OUT), BF16),
        scratch_shapes=(
            [pltpu.VMEM((MXU_N // LANES, CONF_HALO + tq, LANES), F32)] * (CONF_DIM // MXU_N)
            + [pltpu.VMEM((MXU_N // LANES, SC_HALO + tq, LANES), F32)] * (SC_DIM // MXU_N)
            + [pltpu.VMEM((tq, CONF_DIM), F32), pltpu.VMEM((tq, D), BF16)]
        ),
        compiler_params=pltpu.CompilerParams(
            dimension_semantics=("arbitrary", "arbitrary"), vmem_limit_bytes=VMEM_LIMIT),
        name="mixer_cd",
    )(x, *args)


def _kv_kernel(mem_ref, wkv_ref, o_ref):
    o_ref[...] = _dot(mem_ref[...].astype(BF16), wkv_ref[...]).astype(o_ref.dtype)


def _kv_proj(mem, wkv):
    B, M, D = mem.shape
    return pl.pallas_call(
        _kv_kernel,
        grid=(B,),
        in_specs=[pl.BlockSpec((None, M, D), lambda b: (b, 0, 0)), _const_spec(wkv.shape)],
        out_specs=pl.BlockSpec((None, M, 2 * D), lambda b: (b, 0, 0)),
        out_shape=jax.ShapeDtypeStruct((B, M, 2 * D), BF16),
        compiler_params=pltpu.CompilerParams(
            dimension_semantics=("arbitrary",), vmem_limit_bytes=VMEM_LIMIT),
        name="kv_proj",
    )(mem, wkv.astype(BF16))


def _post_kernel(x_ref, mix_ref, kv_ref, g_ref, wout_ref, wq_ref, wo_ref, w1_ref, w2_ref, o_ref):
    tm = x_ref.shape[0]
    x = x_ref[...]

    x = x + _rms(_dot(mix_ref[...], wout_ref[...]), g_ref[1:2, :])

    q = _dot(_rms(x, g_ref[2:3, :]).astype(BF16), wq_ref[...])
    heads = []
    for hd in range(XA_HEADS):
        cols = slice(hd * XA_HEAD_DIM, (hd + 1) * XA_HEAD_DIM)
        kh = kv_ref[:, cols]
        vh = kv_ref[:, D_MODEL + hd * XA_HEAD_DIM:D_MODEL + (hd + 1) * XA_HEAD_DIM]
        s = lax.dot_general(q[:, cols].astype(BF16), kh, (((1,), (1,)), ((), ())),
                            preferred_element_type=F32) * (1.0 / XA_HEAD_DIM ** 0.5)
        e = jnp.exp(s - jnp.max(s, axis=-1, keepdims=True))
        p = e * (1.0 / jnp.sum(e, axis=-1, keepdims=True))
        heads.append(_dot(p.astype(BF16), vh).astype(BF16))
    o = jnp.concatenate(heads, axis=-1)
    x = x + _rms(_dot(o, wo_ref[...]), g_ref[3:4, :])

    h = _rms(x, g_ref[4:5, :]).astype(BF16)
    acc = jnp.zeros((tm, D_MODEL), F32)
    for c in range(0, MLP_HIDDEN, MLP_CHUNK):
        r = jnp.maximum(_dot(h, w1_ref[:, c:c + MLP_CHUNK]), 0.0)
        acc = acc + _dot((r * r).astype(BF16), w2_ref[c:c + MLP_CHUNK, :])
    o_ref[...] = x + _rms(acc, g_ref[5:6, :])


def _post(x, mix, kv, gains, w_out, wq, wo, w1, w2):
    B, S, D = x.shape
    tm = min(SEQ_TILE, S)
    ws = tuple(w.astype(BF16) for w in (w_out, wq, wo, w1, w2))
    return pl.pallas_call(
        _post_kernel,
        grid=(B, S // tm),
        in_specs=[
            pl.BlockSpec((None, tm, D), lambda b, s: (b, s, 0)),
            pl.BlockSpec((None, tm, mix.shape[-1]), lambda b, s: (b, s, 0)),
            pl.BlockSpec((None, kv.shape[1], kv.shape[2]), lambda b, s: (b, 0, 0)),
            _const_spec(gains.shape),
        ] + [_const_spec(w.shape) for w in ws],
        out_specs=pl.BlockSpec((None, tm, D), lambda b, s: (b, s, 0)),
        out_shape=jax.ShapeDtypeStruct((B, S, D), F32),
        compiler_params=pltpu.CompilerParams(
            dimension_semantics=("arbitrary", "arbitrary"), vmem_limit_bytes=VMEM_LIMIT),
        name="post",
    )(x, mix, kv, gains, *ws)


def kernel(x, mem, norm_gains, xa_wq, xa_wkv, xa_wo, mlp_w1, mlp_w2, ab_w_in, pool_w, pool_scale, ssm_conv_w, ssm_conv_b, ssm_dt_bias, ssm_a_log, ssm_d, ssm_norm, ab_w_out, cd_w_in, conf_dw_w, conf_dw_b, conf_ln_g, conf_ln_b, sc_conv_w, cd_w_out):
    depth = norm_gains.shape[0]
    for layer in range(depth):
        g = norm_gains[layer].astype(F32)
        i = layer // 2
        if layer % 2 == 0:
            mix = _mixer_ab(x, g, ab_w_in[i], pool_w[i], pool_scale[i], ssm_conv_w[i], ssm_conv_b[i],
                            ssm_dt_bias[i], ssm_a_log[i], ssm_d[i], ssm_norm[i])
            w_out = ab_w_out[i]
        else:
            mix = _mixer_cd(x, g, cd_w_in[i], conf_dw_w[i], conf_dw_b[i], conf_ln_g[i], conf_ln_b[i],
                            sc_conv_w[i])
            w_out = cd_w_out[i]
        kv = _kv_proj(mem, xa_wkv[layer])
        x = _post(x, mix, kv, g, w_out, xa_wq[layer], xa_wo[layer], mlp_w1[layer], mlp_w2[layer])
    return x
```

```python
import jax
import jax.numpy as jnp
from jax import lax
from jax.experimental import pallas as pl
from jax.experimental.pallas import tpu as pltpu

F32 = jnp.float32
BF16 = jnp.bfloat16

D_MODEL = 1024
N_MEM = 256
XA_HEADS = 4
XA_HEAD_DIM = D_MODEL // XA_HEADS

LANES = 128
MXU_N = 256

POOL_WINDOWS = (2, 4, 8, 16)
POOL_GROUP_DIM = 128
POOL_WIDTH = len(POOL_WINDOWS) * POOL_GROUP_DIM
POOL_HALO = 16

SSM_HEAD_DIM = 64
SSM_INNER = 1024
SSM_HEADS = 16
SSM_GROUPS = 2
SSM_STATE = 128
SSM_CONV = 4
SSM_CONV_HALO = 8
SSM_CHUNK = 128
SSM_GN = SSM_GROUPS * SSM_STATE
SSM_CONV_DIM = SSM_INNER + 2 * SSM_GN
AB_IN = POOL_WIDTH + SSM_INNER + SSM_CONV_DIM + SSM_HEADS
AB_OUT = POOL_WIDTH + SSM_INNER
AB_IN_PAD = POOL_WIDTH + SSM_INNER + SSM_CONV_DIM + LANES
OFF_Z = POOL_WIDTH
OFF_XBC = OFF_Z + SSM_INNER
OFF_DT = OFF_XBC + SSM_CONV_DIM

CONF_DIM = 1024
CONF_KERNEL = 31
CONF_HALO = 32
SC_DIM = 1024
SC_KERNEL = 3
SC_HALO = 8
CD_IN = 2 * CONF_DIM + 3 * SC_DIM
CD_OUT = CONF_DIM + SC_DIM
CONV_ROWS = 64
CD_BLOCKS_PER_GROUP = CD_IN // CONF_DIM

MLP_HIDDEN = 4096
MLP_CHUNK = 512
RMS_EPS = 1e-6
LN_EPS = 1e-5

SEQ_TILE = 512
VMEM_LIMIT = 56 * 1024 * 1024


def _rms(x, g):
    return x * lax.rsqrt(jnp.mean(x * x, axis=-1, keepdims=True) + RMS_EPS) * g


def _silu(x):
    return x * jax.nn.sigmoid(x)


def _dot(a, b):
    return jnp.dot(a, b, preferred_element_type=F32)


def _const_spec(shape):
    nd = len(shape)
    return pl.BlockSpec(shape, lambda *_: (0,) * nd, pipeline_mode=pl.Buffered(1))


def _layer_spec(stack_shape, layer):
    nd = len(stack_shape) - 1
    return pl.BlockSpec((None,) + tuple(stack_shape[1:]), lambda *_: (layer,) + (0,) * nd,
                        pipeline_mode=pl.Buffered(1))


def _store_slabs(slab_ref, first_slab, row0, val):
    rows = val.shape[0]
    for j in range(val.shape[1] // LANES):
        slab_ref[first_slab + j, row0:row0 + rows, :] = val[:, j * LANES:(j + 1) * LANES]


def _dwconv(slab_ref, slab, w_ref, cols, width, halo, row0, rows, init):
    acc = init
    for k in range(width):
        r = halo - (width - 1) + k + row0
        acc = acc + w_ref[k:k + 1, cols] * slab_ref[slab, r:r + rows, :]
    return acc


def _ab_kernel(x_ref, g_ref, win_ref, poolw_ref, pscale_ref, convw_ref, convb_ref, dtb_ref,
               alog_ref, dskip_ref, normw_ref, o_ref, pool_s, xc_s, z_s, state_s):
    tq = x_ref.shape[0]
    s_idx = pl.program_id(1)
    HP, HC, L = POOL_HALO, SSM_CONV_HALO, SSM_CHUNK
    n_chunks = tq // L
    x_slabs = SSM_INNER // LANES
    b_slab0 = x_slabs
    c_slab0 = x_slabs + SSM_GN // LANES

    @pl.when(s_idx == 0)
    def _():
        pool_s[:, 0:HP, :] = jnp.zeros((pool_s.shape[0], HP, LANES), F32)
        xc_s[:, 0:HC, :] = jnp.zeros((xc_s.shape[0], HC, LANES), F32)
        state_s[...] = jnp.zeros_like(state_s)

    h = _rms(x_ref[...], g_ref[0:1, :]).astype(BF16)

    def conv_silu(slab, row0):
        cols = slice(slab * LANES, (slab + 1) * LANES)
        return _silu(_dwconv(xc_s, slab, convw_ref, cols, SSM_CONV, HC, row0, L, convb_ref[:, cols]))

    _store_slabs(pool_s, 0, HP, _dot(h, win_ref[:, 0:POOL_WIDTH]))
    pos = (s_idx * tq + 1 + lax.broadcasted_iota(jnp.int32, (tq, 1), 0)).astype(F32)
    for g, w in enumerate(POOL_WINDOWS):
        c0 = g * POOL_GROUP_DIM
        cur = pool_s[g, HP:HP + tq, :]
        acc = cur
        for j in range(1, w):
            acc = acc + pool_s[g, HP - j:HP - j + tq, :]
        mean = acc / jnp.minimum(pos, float(w))
        po = _dot((mean - cur).astype(BF16), poolw_ref[g]) * pscale_ref[:, c0:c0 + POOL_GROUP_DIM]
        o_ref[:, c0:c0 + POOL_GROUP_DIM] = po.astype(o_ref.dtype)

    a_row = -jnp.exp(alog_ref[...])
    ri = lax.broadcasted_iota(jnp.int32, (L, L), 0)
    ci = lax.broadcasted_iota(jnp.int32, (L, L), 1)
    causal = ri >= ci
    tril = jnp.where(causal, 1.0, 0.0).astype(BF16)
    lo_half = lax.broadcasted_iota(jnp.int32, (1, LANES), 1) < SSM_HEAD_DIM
    dt_raw = _dot(h, win_ref[:, OFF_DT:OFF_DT + LANES])
    dtps, css, cs_ts = [], [], []
    for c in range(n_chunks):
        dtp = jax.nn.softplus(dt_raw[c * L:(c + 1) * L, :] + dtb_ref[...])
        d_a = dtp * a_row
        hi = d_a.astype(BF16)
        r1 = d_a - hi.astype(F32)
        mid = r1.astype(BF16)
        lo = (r1 - mid.astype(F32)).astype(BF16)
        cs = _dot(tril, hi) + _dot(tril, mid) + _dot(tril, lo)
        dtps.append(dtp)
        css.append(cs)
        cs_ts.append(cs.T)

    for c0 in range(0, SSM_CONV_DIM, MXU_N):
        _store_slabs(xc_s, c0 // LANES, HC, _dot(h, win_ref[:, OFF_XBC + c0:OFF_XBC + c0 + MXU_N]))
    for c0 in range(0, SSM_INNER, MXU_N):
        z_s[:, c0:c0 + MXU_N] = _dot(h, win_ref[:, OFF_Z + c0:OFF_Z + c0 + MXU_N])

    slabs_per_group = x_slabs // SSM_GROUPS
    group_width = SSM_INNER // SSM_GROUPS
    for c in range(n_chunks):
        rows = slice(c * L, (c + 1) * L)
        cs, cs_t, dtp = css[c], cs_ts[c], dtps[c]
        for g in range(SSM_GROUPS):
            gcols = slice(g * group_width, (g + 1) * group_width)
            bm = conv_silu(b_slab0 + g, c * L)
            cm_b = conv_silu(c_slab0 + g, c * L).astype(BF16)
            bm_t = bm.T.astype(BF16)
            cb = lax.dot_general(cm_b, bm.astype(BF16), (((1,), (1,)), ((), ())),
                                 preferred_element_type=F32)
            xs_l, xdt_l, cs_l, yd_l = [], [], [], []
            for sl in range(slabs_per_group):
                slab = g * slabs_per_group + sl
                h0 = 2 * slab
                xs = conv_silu(slab, c * L)
                col0 = jnp.broadcast_to(cs[:, h0:h0 + 1], (L, L))
                col1 = jnp.broadcast_to(cs[:, h0 + 1:h0 + 2], (L, L))
                row0 = jnp.broadcast_to(cs_t[h0:h0 + 1, :], (L, L))
                row1 = jnp.broadcast_to(cs_t[h0 + 1:h0 + 2, :], (L, L))
                dt_pair = jnp.where(lo_half, jnp.broadcast_to(dtp[:, h0:h0 + 1], (L, LANES)),
                                    jnp.broadcast_to(dtp[:, h0 + 1:h0 + 2], (L, LANES)))
                xdt = xs * dt_pair
                xdt_b = xdt.astype(BF16)
                zero_b = jnp.zeros_like(xdt_b)
                m0 = (cb * jnp.exp(jnp.where(causal, col0 - row0, -jnp.inf))).astype(BF16)
                m1 = (cb * jnp.exp(jnp.where(causal, col1 - row1, -jnp.inf))).astype(BF16)
                m01 = jnp.concatenate([m0, m1], axis=1)
                x01 = jnp.concatenate([jnp.where(lo_half, xdt_b, zero_b),
                                       jnp.where(lo_half, zero_b, xdt_b)], axis=0)
                yd_l.append(_dot(m01, x01))
                xs_l.append(xs)
                xdt_l.append(xdt)
                cs_l.append(jnp.where(lo_half, col0, col1))
            xs_g = jnp.concatenate(xs_l, axis=1)
            xdt_g = jnp.concatenate(xdt_l, axis=1)
            cs_g = jnp.concatenate(cs_l, axis=1)
            hs = state_s[:, gcols]
            y = jnp.concatenate(yd_l, axis=1) + _dot(cm_b, hs.astype(BF16)) * jnp.exp(cs_g)
            last = cs_g[L - 1:L, :]
            xw = (xdt_g * jnp.exp(last - cs_g)).astype(BF16)
            state_s[:, gcols] = hs * jnp.exp(last) + _dot(bm_t, xw)
            y = y + dskip_ref[:, gcols] * xs_g
            y = y * _silu(z_s[rows, gcols])
            scale = lax.rsqrt(jnp.mean(y * y, axis=-1, keepdims=True) + RMS_EPS)
            yo = y * scale * normw_ref[:, gcols]
            o_ref[rows, POOL_WIDTH + g * group_width:POOL_WIDTH + (g + 1) * group_width] = yo.astype(o_ref.dtype)

    pool_s[:, 0:HP, :] = pool_s[:, tq:tq + HP, :]
    xc_s[:, 0:HC, :] = xc_s[:, tq:tq + HC, :]


def _mixer_ab(x, gains, w_in, pool_w, pool_scale, conv_w, conv_b, dt_bias, a_log, d_skip, norm_w):
    B, S, D = x.shape
    tq = min(SEQ_TILE, S)
    pad = AB_IN_PAD - AB_IN
    win = jnp.pad(w_in, ((0, 0), (0, pad))).astype(BF16)
    row = lambda v: v.reshape(1, -1).astype(F32)
    lane_pad = lambda v: jnp.pad(v.astype(F32), (0, LANES - v.shape[0])).reshape(1, LANES)
    args = (
        gains, win, pool_w.astype(BF16), row(pool_scale), conv_w.astype(F32), row(conv_b),
        lane_pad(dt_bias), lane_pad(a_log), row(jnp.repeat(d_skip, SSM_HEAD_DIM)), row(norm_w),
    )
    return pl.pallas_call(
        _ab_kernel,
        grid=(B, S // tq),
        in_specs=[pl.BlockSpec((None, tq, D), lambda b, s: (b, s, 0))] + [_const_spec(a.shape) for a in args],
        out_specs=pl.BlockSpec((None, tq, AB_OUT), lambda b, s: (b, s, 0)),
        out_shape=jax.ShapeDtypeStruct((B, S, AB_OUT), BF16),
        scratch_shapes=[
            pltpu.VMEM((POOL_WIDTH // LANES, POOL_HALO + tq, LANES), F32),
            pltpu.VMEM((SSM_CONV_DIM // LANES, SSM_CONV_HALO + tq, LANES), F32),
            pltpu.VMEM((tq, SSM_INNER), F32),
            pltpu.VMEM((SSM_STATE, SSM_INNER), F32),
        ],
        compiler_params=pltpu.CompilerParams(
            dimension_semantics=("arbitrary", "arbitrary"), vmem_limit_bytes=VMEM_LIMIT),
        name="mixer_ab",
    )(x, *args)


def _cd_kernel(x_ref, g_ref, win_ref, dww_ref, dwb_ref, lng_ref, lnb_ref, scw_ref, o_ref, *scratch):
    tq = x_ref.shape[0]
    s_idx = pl.program_id(1)
    HV, HP = CONF_HALO, SC_HALO
    slabs_per_col_group = MXU_N // LANES
    n_groups = CONF_DIM // MXU_N
    v_grp = scratch[0:n_groups]
    p_grp = scratch[n_groups:2 * n_groups]
    c_s, h_s = scratch[2 * n_groups:]

    @pl.when(s_idx == 0)
    def _():
        for v_s in v_grp:
            v_s[:, 0:HV, :] = jnp.zeros((slabs_per_col_group, HV, LANES), F32)
        for p_s in p_grp:
            p_s[:, 0:HP, :] = jnp.zeros((slabs_per_col_group, HP, LANES), F32)


    def project(cg, blk):
        base = (cg * CD_BLOCKS_PER_GROUP + blk) * MXU_N
        return _dot(h_s[...], win_ref[:, base:base + MXU_N])

    def project_glu(cg):
        for j in range(slabs_per_col_group):
            r = project(cg, j)
            v_grp[cg][j, HV:HV + tq, :] = r[:, 0:LANES] * jax.nn.sigmoid(r[:, LANES:])

    def short_conv(cg):
        for j in range(slabs_per_col_group):
            r = project(cg, slabs_per_col_group + j)
            p_grp[cg][j, HP:HP + tq, :] = r[:, 0:LANES] * r[:, LANES:]
        bg = project(cg, 2 * slabs_per_col_group)
        for j in range(slabs_per_col_group):
            slab = cg * slabs_per_col_group + j
            cols = slice(slab * LANES, (slab + 1) * LANES)
            conv = _dwconv(p_grp[cg], j, scw_ref, cols, SC_KERNEL, HP, 0, tq, jnp.zeros((1, LANES), F32))
            o_ref[:, CONF_DIM + slab * LANES:CONF_DIM + (slab + 1) * LANES] = (
                bg[:, j * LANES:(j + 1) * LANES] * conv).astype(o_ref.dtype)

    def conv31_block(cg, blk):
        j, rb = divmod(blk, tq // CONV_ROWS)
        slab = cg * slabs_per_col_group + j
        cols = slice(slab * LANES, (slab + 1) * LANES)
        r0 = rb * CONV_ROWS
        c_s[r0:r0 + CONV_ROWS, cols] = _dwconv(v_grp[cg], j, dww_ref, cols, CONF_KERNEL, HV, r0,
                                               CONV_ROWS, dwb_ref[:, cols])

    blocks_per_group = slabs_per_col_group * (tq // CONV_ROWS)
    third = blocks_per_group // 3

    h_s[...] = _rms(x_ref[...], g_ref[0:1, :]).astype(BF16)
    project_glu(0)
    for cg in range(n_groups):
        for blk in range(0, third):
            conv31_block(cg, blk)
        if cg + 1 < n_groups:
            project_glu(cg + 1)
        for blk in range(third, 2 * third):
            conv31_block(cg, blk)
        short_conv(cg)
        for blk in range(2 * third, blocks_per_group):
            conv31_block(cg, blk)

    for rb in range(tq // CONV_ROWS):
        rows = slice(rb * CONV_ROWS, (rb + 1) * CONV_ROWS)
        acc = c_s[rows, :]
        mu = jnp.mean(acc, axis=-1, keepdims=True)
        cen = acc - mu
        var = jnp.mean(cen * cen, axis=-1, keepdims=True)
        vn = cen * lax.rsqrt(var + LN_EPS) * lng_ref[...] + lnb_ref[...]
        o_ref[rows, 0:CONF_DIM] = _silu(vn).astype(o_ref.dtype)

    for v_s in v_grp:
        v_s[:, 0:HV, :] = v_s[:, tq:tq + HV, :]
    for p_s in p_grp:
        p_s[:, 0:HP, :] = p_s[:, tq:tq + HP, :]


def _cd_weight_layout(w_in):
    val, gate, bg, cgate, hh = range(5)

    def tile(part, s):
        c0 = part * CONF_DIM + s * LANES
        return w_in[:, c0:c0 + LANES]

    cols = []
    for cg in range(CONF_DIM // MXU_N):
        a, b = 2 * cg, 2 * cg + 1
        cols += [tile(val, a), tile(gate, a), tile(val, b), tile(gate, b),
                 tile(cgate, a), tile(hh, a), tile(cgate, b), tile(hh, b), tile(bg, a), tile(bg, b)]
    return jnp.concatenate(cols, axis=1)


def _mixer_cd(x, gains, w_in, dw_w, dw_b, ln_g, ln_b, sc_w):
    B, S, D = x.shape
    tq = min(SEQ_TILE, S)
    row = lambda v: v.reshape(1, -1).astype(F32)
    args = (gains, _cd_weight_layout(w_in).astype(BF16), dw_w.astype(F32), row(dw_b), row(ln_g), row(ln_b),
            sc_w.astype(F32))
    return pl.pallas_call(
        _cd_kernel,
        grid=(B, S // tq),
        in_specs=[pl.BlockSpec((None, tq, D), lambda b, s: (b, s, 0))] + [_const_spec(a.shape) for a in args],
        out_specs=pl.BlockSpec((None, tq, CD_OUT), lambda b, s: (b, s, 0)),
        out_shape=jax.ShapeDtypeStruct((B, S, CD_OUT), BF16),
        scratch_shapes=(
            [pltpu.VMEM((MXU_N // LANES, CONF_HALO + tq, LANES), F32)] * (CONF_DIM // MXU_N)
            + [pltpu.VMEM((MXU_N // LANES, SC_HALO + tq, LANES), F32)] * (SC_DIM // MXU_N)
            + [pltpu.VMEM((tq, CONF_DIM), F32), pltpu.VMEM((tq, D), BF16)]
        ),
        compiler_params=pltpu.CompilerParams(
            dimension_semantics=("arbitrary", "arbitrary"), vmem_limit_bytes=VMEM_LIMIT),
        name="mixer_cd",
    )(x, *args)


def _kv_kernel(mem_ref, wkv_ref, o_ref):
    o_ref[...] = _dot(mem_ref[...].astype(BF16), wkv_ref[...]).astype(o_ref.dtype)


def _kv_proj(mem, wkv):
    B, M, D = mem.shape
    depth = wkv.shape[0]
    return pl.pallas_call(
        _kv_kernel,
        grid=(depth, B),
        in_specs=[pl.BlockSpec((None, M, D), lambda l, b: (b, 0, 0)),
                  pl.BlockSpec((None, D, 2 * D), lambda l, b: (l, 0, 0))],
        out_specs=pl.BlockSpec((None, None, M, 2 * D), lambda l, b: (l, b, 0, 0)),
        out_shape=jax.ShapeDtypeStruct((depth, B, M, 2 * D), BF16),
        compiler_params=pltpu.CompilerParams(
            dimension_semantics=("arbitrary", "arbitrary"), vmem_limit_bytes=VMEM_LIMIT),
        name="kv_proj",
    )(mem, wkv.astype(BF16))


def _post_kernel(x_ref, mix_ref, kv_ref, g_ref, wout_ref, wq_ref, wo_ref, w1_ref, w2_ref, o_ref):
    tm = x_ref.shape[0]
    x = x_ref[...]

    x = x + _rms(_dot(mix_ref[...], wout_ref[...]), g_ref[1:2, :])

    q = _dot(_rms(x, g_ref[2:3, :]).astype(BF16), wq_ref[...])
    heads = []
    for hd in range(XA_HEADS):
        cols = slice(hd * XA_HEAD_DIM, (hd + 1) * XA_HEAD_DIM)
        kh = kv_ref[:, cols]
        vh = kv_ref[:, D_MODEL + hd * XA_HEAD_DIM:D_MODEL + (hd + 1) * XA_HEAD_DIM]
        s = lax.dot_general(q[:, cols].astype(BF16), kh, (((1,), (1,)), ((), ())),
                            preferred_element_type=F32) * (1.0 / XA_HEAD_DIM ** 0.5)
        e = jnp.exp(s - jnp.max(s, axis=-1, keepdims=True))
        p = e * (1.0 / jnp.sum(e, axis=-1, keepdims=True))
        heads.append(_dot(p.astype(BF16), vh).astype(BF16))
    o = jnp.concatenate(heads, axis=-1)
    x = x + _rms(_dot(o, wo_ref[...]), g_ref[3:4, :])

    h = _rms(x, g_ref[4:5, :]).astype(BF16)
    acc = jnp.zeros((tm, D_MODEL), F32)
    for c in range(0, MLP_HIDDEN, MLP_CHUNK):
        r = jnp.maximum(_dot(h, w1_ref[:, c:c + MLP_CHUNK]), 0.0)
        acc = acc + _dot((r * r).astype(BF16), w2_ref[c:c + MLP_CHUNK, :])
    o_ref[...] = x + _rms(acc, g_ref[5:6, :])


def _post(layer, x, mix, kv, gains, w_out, wq, wo, w1, w2):
    B, S, D = x.shape
    tm = min(SEQ_TILE, S)
    w_out = w_out.astype(BF16)
    stacks = (gains, wq, wo, w1, w2)
    return pl.pallas_call(
        _post_kernel,
        grid=(B, S // tm),
        in_specs=[
            pl.BlockSpec((None, tm, D), lambda b, s: (b, s, 0)),
            pl.BlockSpec((None, tm, mix.shape[-1]), lambda b, s: (b, s, 0)),
            pl.BlockSpec((None, None, kv.shape[2], kv.shape[3]), lambda b, s: (layer, b, 0, 0)),
            _layer_spec(gains.shape, layer),
            _const_spec(w_out.shape),
        ] + [_layer_spec(w.shape, layer) for w in stacks[1:]],
        out_specs=pl.BlockSpec((None, tm, D), lambda b, s: (b, s, 0)),
        out_shape=jax.ShapeDtypeStruct((B, S, D), F32),
        compiler_params=pltpu.CompilerParams(
            dimension_semantics=("arbitrary", "arbitrary"), vmem_limit_bytes=VMEM_LIMIT),
        name="post",
    )(x, mix, kv, gains, w_out, *stacks[1:])


def kernel(x, mem, norm_gains, xa_wq, xa_wkv, xa_wo, mlp_w1, mlp_w2, ab_w_in, pool_w, pool_scale, ssm_conv_w, ssm_conv_b, ssm_dt_bias, ssm_a_log, ssm_d, ssm_norm, ab_w_out, cd_w_in, conf_dw_w, conf_dw_b, conf_ln_g, conf_ln_b, sc_conv_w, cd_w_out):
    depth = norm_gains.shape[0]
    gains = norm_gains.astype(F32)
    wq, wo, w1, w2 = (w.astype(BF16) for w in (xa_wq, xa_wo, mlp_w1, mlp_w2))
    kv = _kv_proj(mem, xa_wkv)
    for layer in range(depth):
        g = gains[layer]
        i = layer // 2
        if layer % 2 == 0:
            mix = _mixer_ab(x, g, ab_w_in[i], pool_w[i], pool_scale[i], ssm_conv_w[i], ssm_conv_b[i],
                            ssm_dt_bias[i], ssm_a_log[i], ssm_d[i], ssm_norm[i])
            w_out = ab_w_out[i]
        else:
            mix = _mixer_cd(x, g, cd_w_in[i], conf_dw_w[i], conf_dw_b[i], conf_ln_g[i], conf_ln_b[i],
                            sc_conv_w[i])
            w_out = cd_w_out[i]
        x = _post(layer, x, mix, kv, gains, w_out, wq, wo, w1, w2)
    return x
```

```python
import jax
import jax.numpy as jnp
from jax import lax
from jax.experimental import pallas as pl
from jax.experimental.pallas import tpu as pltpu

F32 = jnp.float32
BF16 = jnp.bfloat16

D_MODEL = 1024
N_MEM = 256
XA_HEADS = 4
XA_HEAD_DIM = D_MODEL // XA_HEADS

LANES = 128
MXU_N = 256

POOL_WINDOWS = (2, 4, 8, 16)
POOL_GROUP_DIM = 128
POOL_WIDTH = len(POOL_WINDOWS) * POOL_GROUP_DIM
POOL_HALO = 16

SSM_HEAD_DIM = 64
SSM_INNER = 1024
SSM_HEADS = 16
SSM_GROUPS = 2
SSM_STATE = 128
SSM_CONV = 4
SSM_CONV_HALO = 8
SSM_CHUNK = 128
SSM_GN = SSM_GROUPS * SSM_STATE
SSM_CONV_DIM = SSM_INNER + 2 * SSM_GN
AB_IN = POOL_WIDTH + SSM_INNER + SSM_CONV_DIM + SSM_HEADS
AB_OUT = POOL_WIDTH + SSM_INNER
AB_IN_PAD = POOL_WIDTH + SSM_INNER + SSM_CONV_DIM + LANES
OFF_Z = POOL_WIDTH
OFF_XBC = OFF_Z + SSM_INNER
OFF_DT = OFF_XBC + SSM_CONV_DIM

CONF_DIM = 1024
CONF_KERNEL = 31
CONF_HALO = 32
SC_DIM = 1024
SC_KERNEL = 3
SC_HALO = 8
CD_IN = 2 * CONF_DIM + 3 * SC_DIM
CD_OUT = CONF_DIM + SC_DIM
CONV_ROWS = 64
CD_BLOCKS_PER_GROUP = CD_IN // CONF_DIM

MLP_HIDDEN = 4096
MLP_CHUNK = 512
RMS_EPS = 1e-6
LN_EPS = 1e-5

SEQ_TILE = 512
POST_TILE = 1024
ATTN_ROW_SPLIT = 2
VMEM_LIMIT = 56 * 1024 * 1024


def _rms(x, g):
    return x * lax.rsqrt(jnp.mean(x * x, axis=-1, keepdims=True) + RMS_EPS) * g


def _silu(x):
    return x * jax.nn.sigmoid(x)


def _dot(a, b):
    return jnp.dot(a, b, preferred_element_type=F32)


def _const_spec(shape):
    nd = len(shape)
    return pl.BlockSpec(shape, lambda *_: (0,) * nd, pipeline_mode=pl.Buffered(1))


def _layer_spec(stack_shape, layer):
    nd = len(stack_shape) - 1
    return pl.BlockSpec((None,) + tuple(stack_shape[1:]), lambda *_: (layer,) + (0,) * nd,
                        pipeline_mode=pl.Buffered(1))


def _store_slabs(slab_ref, first_slab, row0, val):
    rows = val.shape[0]
    for j in range(val.shape[1] // LANES):
        slab_ref[first_slab + j, row0:row0 + rows, :] = val[:, j * LANES:(j + 1) * LANES]


def _dwconv(slab_ref, slab, w_ref, cols, width, halo, row0, rows, init):
    acc = init
    for k in range(width):
        r = halo - (width - 1) + k + row0
        acc = acc + w_ref[k:k + 1, cols] * slab_ref[slab, r:r + rows, :]
    return acc


def _ab_kernel(x_ref, g_ref, win_ref, poolw_ref, pscale_ref, convw_ref, convb_ref, dtb_ref,
               alog_ref, dskip_ref, normw_ref, o_ref, pool_s, xc_s, z_s, state_s):
    tq = x_ref.shape[0]
    s_idx = pl.program_id(1)
    HP, HC, L = POOL_HALO, SSM_CONV_HALO, SSM_CHUNK
    n_chunks = tq // L
    x_slabs = SSM_INNER // LANES
    b_slab0 = x_slabs
    c_slab0 = x_slabs + SSM_GN // LANES

    @pl.when(s_idx == 0)
    def _():
        pool_s[:, 0:HP, :] = jnp.zeros((pool_s.shape[0], HP, LANES), F32)
        xc_s[:, 0:HC, :] = jnp.zeros((xc_s.shape[0], HC, LANES), F32)
        state_s[...] = jnp.zeros_like(state_s)

    h = _rms(x_ref[...], g_ref[0:1, :]).astype(BF16)

    def conv_silu(slab, row0):
        cols = slice(slab * LANES, (slab + 1) * LANES)
        return _silu(_dwconv(xc_s, slab, convw_ref, cols, SSM_CONV, HC, row0, L, convb_ref[:, cols]))

    _store_slabs(pool_s, 0, HP, _dot(h, win_ref[:, 0:POOL_WIDTH]))
    pos = (s_idx * tq + 1 + lax.broadcasted_iota(jnp.int32, (tq, 1), 0)).astype(F32)
    for g, w in enumerate(POOL_WINDOWS):
        c0 = g * POOL_GROUP_DIM
        cur = pool_s[g, HP:HP + tq, :]
        acc = cur
        for j in range(1, w):
            acc = acc + pool_s[g, HP - j:HP - j + tq, :]
        mean = acc / jnp.minimum(pos, float(w))
        po = _dot((mean - cur).astype(BF16), poolw_ref[g]) * pscale_ref[:, c0:c0 + POOL_GROUP_DIM]
        o_ref[:, c0:c0 + POOL_GROUP_DIM] = po.astype(o_ref.dtype)

    a_row = -jnp.exp(alog_ref[...])
    ri = lax.broadcasted_iota(jnp.int32, (L, L), 0)
    ci = lax.broadcasted_iota(jnp.int32, (L, L), 1)
    causal = ri >= ci
    tril = jnp.where(causal, 1.0, 0.0).astype(BF16)
    lo_half = lax.broadcasted_iota(jnp.int32, (1, LANES), 1) < SSM_HEAD_DIM
    dt_raw = _dot(h, win_ref[:, OFF_DT:OFF_DT + LANES])
    dtps, css, cs_ts = [], [], []
    for c in range(n_chunks):
        dtp = jax.nn.softplus(dt_raw[c * L:(c + 1) * L, :] + dtb_ref[...])
        d_a = dtp * a_row
        hi = d_a.astype(BF16)
        r1 = d_a - hi.astype(F32)
        mid = r1.astype(BF16)
        lo = (r1 - mid.astype(F32)).astype(BF16)
        cs = _dot(tril, hi) + _dot(tril, mid) + _dot(tril, lo)
        dtps.append(dtp)
        css.append(cs)
        cs_ts.append(cs.T)

    for c0 in range(0, SSM_CONV_DIM, MXU_N):
        _store_slabs(xc_s, c0 // LANES, HC, _dot(h, win_ref[:, OFF_XBC + c0:OFF_XBC + c0 + MXU_N]))
    for c0 in range(0, SSM_INNER, MXU_N):
        z_s[:, c0:c0 + MXU_N] = _dot(h, win_ref[:, OFF_Z + c0:OFF_Z + c0 + MXU_N])

    slabs_per_group = x_slabs // SSM_GROUPS
    group_width = SSM_INNER // SSM_GROUPS
    for c in range(n_chunks):
        rows = slice(c * L, (c + 1) * L)
        cs, cs_t, dtp = css[c], cs_ts[c], dtps[c]
        for g in range(SSM_GROUPS):
            gcols = slice(g * group_width, (g + 1) * group_width)
            bm = conv_silu(b_slab0 + g, c * L)
            cm_b = conv_silu(c_slab0 + g, c * L).astype(BF16)
            bm_t = bm.T.astype(BF16)
            cb = lax.dot_general(cm_b, bm.astype(BF16), (((1,), (1,)), ((), ())),
                                 preferred_element_type=F32)
            xs_l, xdt_l, cs_l, yd_l = [], [], [], []
            for sl in range(slabs_per_group):
                slab = g * slabs_per_group + sl
                h0 = 2 * slab
                xs = conv_silu(slab, c * L)
                col0 = jnp.broadcast_to(cs[:, h0:h0 + 1], (L, L))
                col1 = jnp.broadcast_to(cs[:, h0 + 1:h0 + 2], (L, L))
                row0 = jnp.broadcast_to(cs_t[h0:h0 + 1, :], (L, L))
                row1 = jnp.broadcast_to(cs_t[h0 + 1:h0 + 2, :], (L, L))
                dt_pair = jnp.where(lo_half, jnp.broadcast_to(dtp[:, h0:h0 + 1], (L, LANES)),
                                    jnp.broadcast_to(dtp[:, h0 + 1:h0 + 2], (L, LANES)))
                xdt = xs * dt_pair
                xdt_b = xdt.astype(BF16)
                zero_b = jnp.zeros_like(xdt_b)
                m0 = (cb * jnp.exp(jnp.where(causal, col0 - row0, -jnp.inf))).astype(BF16)
                m1 = (cb * jnp.exp(jnp.where(causal, col1 - row1, -jnp.inf))).astype(BF16)
                m01 = jnp.concatenate([m0, m1], axis=1)
                x01 = jnp.concatenate([jnp.where(lo_half, xdt_b, zero_b),
                                       jnp.where(lo_half, zero_b, xdt_b)], axis=0)
                yd_l.append(_dot(m01, x01))
                xs_l.append(xs)
                xdt_l.append(xdt)
                cs_l.append(jnp.where(lo_half, col0, col1))
            xs_g = jnp.concatenate(xs_l, axis=1)
            xdt_g = jnp.concatenate(xdt_l, axis=1)
            cs_g = jnp.concatenate(cs_l, axis=1)
            hs = state_s[:, gcols]
            y = jnp.concatenate(yd_l, axis=1) + _dot(cm_b, hs.astype(BF16)) * jnp.exp(cs_g)
            last = cs_g[L - 1:L, :]
            xw = (xdt_g * jnp.exp(last - cs_g)).astype(BF16)
            state_s[:, gcols] = hs * jnp.exp(last) + _dot(bm_t, xw)
            y = y + dskip_ref[:, gcols] * xs_g
            y = y * _silu(z_s[rows, gcols])
            scale = lax.rsqrt(jnp.mean(y * y, axis=-1, keepdims=True) + RMS_EPS)
            yo = y * scale * normw_ref[:, gcols]
            o_ref[rows, POOL_WIDTH + g * group_width:POOL_WIDTH + (g + 1) * group_width] = yo.astype(o_ref.dtype)

    pool_s[:, 0:HP, :] = pool_s[:, tq:tq + HP, :]
    xc_s[:, 0:HC, :] = xc_s[:, tq:tq + HC, :]


def _mixer_ab(x, gains, w_in, pool_w, pool_scale, conv_w, conv_b, dt_bias, a_log, d_skip, norm_w):
    B, S, D = x.shape
    tq = min(SEQ_TILE, S)
    pad = AB_IN_PAD - AB_IN
    win = jnp.pad(w_in, ((0, 0), (0, pad))).astype(BF16)
    row = lambda v: v.reshape(1, -1).astype(F32)
    lane_pad = lambda v: jnp.pad(v.astype(F32), (0, LANES - v.shape[0])).reshape(1, LANES)
    args = (
        gains, win, pool_w.astype(BF16), row(pool_scale), conv_w.astype(F32), row(conv_b),
        lane_pad(dt_bias), lane_pad(a_log), row(jnp.repeat(d_skip, SSM_HEAD_DIM)), row(norm_w),
    )
    return pl.pallas_call(
        _ab_kernel,
        grid=(B, S // tq),
        in_specs=[pl.BlockSpec((None, tq, D), lambda b, s: (b, s, 0))] + [_const_spec(a.shape) for a in args],
        out_specs=pl.BlockSpec((None, tq, AB_OUT), lambda b, s: (b, s, 0)),
        out_shape=jax.ShapeDtypeStruct((B, S, AB_OUT), BF16),
        scratch_shapes=[
            pltpu.VMEM((POOL_WIDTH // LANES, POOL_HALO + tq, LANES), F32),
            pltpu.VMEM((SSM_CONV_DIM // LANES, SSM_CONV_HALO + tq, LANES), F32),
            pltpu.VMEM((tq, SSM_INNER), F32),
            pltpu.VMEM((SSM_STATE, SSM_INNER), F32),
        ],
        compiler_params=pltpu.CompilerParams(
            dimension_semantics=("arbitrary", "arbitrary"), vmem_limit_bytes=VMEM_LIMIT),
        name="mixer_ab",
    )(x, *args)


def _cd_kernel(x_ref, g_ref, win_ref, dww_ref, dwb_ref, lng_ref, lnb_ref, scw_ref, o_ref, *scratch):
    tq = x_ref.shape[0]
    s_idx = pl.program_id(1)
    HV, HP = CONF_HALO, SC_HALO
    slabs_per_col_group = MXU_N // LANES
    n_groups = CONF_DIM // MXU_N
    v_grp = scratch[0:n_groups]
    p_grp = scratch[n_groups:2 * n_groups]
    c_s, h_s = scratch[2 * n_groups:]

    @pl.when(s_idx == 0)
    def _():
        for v_s in v_grp:
            v_s[:, 0:HV, :] = jnp.zeros((slabs_per_col_group, HV, LANES), F32)
        for p_s in p_grp:
            p_s[:, 0:HP, :] = jnp.zeros((slabs_per_col_group, HP, LANES), F32)


    def project(cg, blk):
        base = (cg * CD_BLOCKS_PER_GROUP + blk) * MXU_N
        return _dot(h_s[...], win_ref[:, base:base + MXU_N])

    def project_glu(cg):
        for j in range(slabs_per_col_group):
            r = project(cg, j)
            v_grp[cg][j, HV:HV + tq, :] = r[:, 0:LANES] * jax.nn.sigmoid(r[:, LANES:])

    def short_conv(cg):
        for j in range(slabs_per_col_group):
            r = project(cg, slabs_per_col_group + j)
            p_grp[cg][j, HP:HP + tq, :] = r[:, 0:LANES] * r[:, LANES:]
        bg = project(cg, 2 * slabs_per_col_group)
        for j in range(slabs_per_col_group):
            slab = cg * slabs_per_col_group + j
            cols = slice(slab * LANES, (slab + 1) * LANES)
            conv = _dwconv(p_grp[cg], j, scw_ref, cols, SC_KERNEL, HP, 0, tq, jnp.zeros((1, LANES), F32))
            o_ref[:, CONF_DIM + slab * LANES:CONF_DIM + (slab + 1) * LANES] = (
                bg[:, j * LANES:(j + 1) * LANES] * conv).astype(o_ref.dtype)

    def conv31_block(cg, blk):
        j, rb = divmod(blk, tq // CONV_ROWS)
        slab = cg * slabs_per_col_group + j
        cols = slice(slab * LANES, (slab + 1) * LANES)
        r0 = rb * CONV_ROWS
        c_s[r0:r0 + CONV_ROWS, cols] = _dwconv(v_grp[cg], j, dww_ref, cols, CONF_KERNEL, HV, r0,
                                               CONV_ROWS, dwb_ref[:, cols])

    blocks_per_group = slabs_per_col_group * (tq // CONV_ROWS)
    third = blocks_per_group // 3

    h_s[...] = _rms(x_ref[...], g_ref[0:1, :]).astype(BF16)
    project_glu(0)
    for cg in range(n_groups):
        for blk in range(0, third):
            conv31_block(cg, blk)
        if cg + 1 < n_groups:
            project_glu(cg + 1)
        for blk in range(third, 2 * third):
            conv31_block(cg, blk)
        short_conv(cg)
        for blk in range(2 * third, blocks_per_group):
            conv31_block(cg, blk)

    for rb in range(tq // CONV_ROWS):
        rows = slice(rb * CONV_ROWS, (rb + 1) * CONV_ROWS)
        acc = c_s[rows, :]
        mu = jnp.mean(acc, axis=-1, keepdims=True)
        cen = acc - mu
        var = jnp.mean(cen * cen, axis=-1, keepdims=True)
        vn = cen * lax.rsqrt(var + LN_EPS) * lng_ref[...] + lnb_ref[...]
        o_ref[rows, 0:CONF_DIM] = _silu(vn).astype(o_ref.dtype)

    for v_s in v_grp:
        v_s[:, 0:HV, :] = v_s[:, tq:tq + HV, :]
    for p_s in p_grp:
        p_s[:, 0:HP, :] = p_s[:, tq:tq + HP, :]


def _cd_weight_layout(w_in):
    val, gate, bg, cgate, hh = range(5)

    def tile(part, s):
        c0 = part * CONF_DIM + s * LANES
        return w_in[:, c0:c0 + LANES]

    cols = []
    for cg in range(CONF_DIM // MXU_N):
        a, b = 2 * cg, 2 * cg + 1
        cols += [tile(val, a), tile(gate, a), tile(val, b), tile(gate, b),
                 tile(cgate, a), tile(hh, a), tile(cgate, b), tile(hh, b), tile(bg, a), tile(bg, b)]
    return jnp.concatenate(cols, axis=1)


def _mixer_cd(x, gains, w_in, dw_w, dw_b, ln_g, ln_b, sc_w):
    B, S, D = x.shape
    tq = min(SEQ_TILE, S)
    row = lambda v: v.reshape(1, -1).astype(F32)
    args = (gains, _cd_weight_layout(w_in).astype(BF16), dw_w.astype(F32), row(dw_b), row(ln_g), row(ln_b),
            sc_w.astype(F32))
    return pl.pallas_call(
        _cd_kernel,
        grid=(B, S // tq),
        in_specs=[pl.BlockSpec((None, tq, D), lambda b, s: (b, s, 0))] + [_const_spec(a.shape) for a in args],
        out_specs=pl.BlockSpec((None, tq, CD_OUT), lambda b, s: (b, s, 0)),
        out_shape=jax.ShapeDtypeStruct((B, S, CD_OUT), BF16),
        scratch_shapes=(
            [pltpu.VMEM((MXU_N // LANES, CONF_HALO + tq, LANES), F32)] * (CONF_DIM // MXU_N)
            + [pltpu.VMEM((MXU_N // LANES, SC_HALO + tq, LANES), F32)] * (SC_DIM // MXU_N)
            + [pltpu.VMEM((tq, CONF_DIM), F32), pltpu.VMEM((tq, D), BF16)]
        ),
        compiler_params=pltpu.CompilerParams(
            dimension_semantics=("arbitrary", "arbitrary"), vmem_limit_bytes=VMEM_LIMIT),
        name="mixer_cd",
    )(x, *args)


def _kv_kernel(mem_ref, wkv_ref, o_ref):
    o_ref[...] = _dot(mem_ref[...].astype(BF16), wkv_ref[...]).astype(o_ref.dtype)


def _kv_proj(mem, wkv):
    B, M, D = mem.shape
    depth = wkv.shape[0]
    return pl.pallas_call(
        _kv_kernel,
        grid=(depth, B),
        in_specs=[pl.BlockSpec((None, M, D), lambda l, b: (b, 0, 0)),
                  pl.BlockSpec((None, D, 2 * D), lambda l, b: (l, 0, 0))],
        out_specs=pl.BlockSpec((None, None, M, 2 * D), lambda l, b: (l, b, 0, 0)),
        out_shape=jax.ShapeDtypeStruct((depth, B, M, 2 * D), BF16),
        compiler_params=pltpu.CompilerParams(
            dimension_semantics=("arbitrary", "arbitrary"), vmem_limit_bytes=VMEM_LIMIT),
        name="kv_proj",
    )(mem, wkv.astype(BF16))


def _attn_kernel(x_ref, mix_ref, kv_ref, g_ref, wout_ref, wq_ref, wo_ref, o_ref):
    tm = x_ref.shape[0]
    half = tm // ATTN_ROW_SPLIT
    halves = [slice(i * half, (i + 1) * half) for i in range(ATTN_ROW_SPLIT)]

    xs = [x_ref[r, :] + _rms(_dot(mix_ref[r, :], wout_ref[...]), g_ref[1:2, :]) for r in halves]

    qs = [_dot(_rms(x, g_ref[2:3, :]).astype(BF16), wq_ref[...]) for x in xs]
    outs = []
    for q in qs:
        heads = []
        for hd in range(XA_HEADS):
            cols = slice(hd * XA_HEAD_DIM, (hd + 1) * XA_HEAD_DIM)
            kh = kv_ref[:, cols]
            vh = kv_ref[:, D_MODEL + hd * XA_HEAD_DIM:D_MODEL + (hd + 1) * XA_HEAD_DIM]
            s = lax.dot_general(q[:, cols].astype(BF16), kh, (((1,), (1,)), ((), ())),
                                preferred_element_type=F32) * (1.0 / XA_HEAD_DIM ** 0.5)
            e = jnp.exp(s - jnp.max(s, axis=-1, keepdims=True))
            p = e * (1.0 / jnp.sum(e, axis=-1, keepdims=True))
            heads.append(_dot(p.astype(BF16), vh).astype(BF16))
        outs.append(jnp.concatenate(heads, axis=-1))
    for r, x, o in zip(halves, xs, outs):
        o_ref[r, :] = x + _rms(_dot(o, wo_ref[...]), g_ref[3:4, :])


def _mlp_kernel(x_ref, g_ref, w1_ref, w2_ref, o_ref):
    tm = x_ref.shape[0]
    x = x_ref[...]
    h = _rms(x, g_ref[4:5, :]).astype(BF16)
    acc = jnp.zeros((tm, D_MODEL), F32)
    for c in range(0, MLP_HIDDEN, MLP_CHUNK):
        r = jnp.maximum(_dot(h, w1_ref[:, c:c + MLP_CHUNK]), 0.0)
        acc = acc + _dot((r * r).astype(BF16), w2_ref[c:c + MLP_CHUNK, :])
    o_ref[...] = x + _rms(acc, g_ref[5:6, :])


def _post(layer, x, mix, kv, gains, w_out, wq, wo, w1, w2):
    B, S, D = x.shape
    tm = min(POST_TILE, S)
    w_out = w_out.astype(BF16)
    tile = lambda width: pl.BlockSpec((None, tm, width), lambda b, s: (b, s, 0))
    params = pltpu.CompilerParams(dimension_semantics=("arbitrary", "arbitrary"), vmem_limit_bytes=VMEM_LIMIT)
    x = pl.pallas_call(
        _attn_kernel,
        grid=(B, S // tm),
        in_specs=[
            tile(D), tile(mix.shape[-1]),
            pl.BlockSpec((None, None, kv.shape[2], kv.shape[3]), lambda b, s: (layer, b, 0, 0)),
            _layer_spec(gains.shape, layer), _const_spec(w_out.shape),
            _layer_spec(wq.shape, layer), _layer_spec(wo.shape, layer),
        ],
        out_specs=tile(D),
        out_shape=jax.ShapeDtypeStruct((B, S, D), F32),
        compiler_params=params,
        name="attn",
    )(x, mix, kv, gains, w_out, wq, wo)
    return pl.pallas_call(
        _mlp_kernel,
        grid=(B, S // tm),
        in_specs=[tile(D), _layer_spec(gains.shape, layer), _layer_spec(w1.shape, layer),
                  _layer_spec(w2.shape, layer)],
        out_specs=tile(D),
        out_shape=jax.ShapeDtypeStruct((B, S, D), F32),
        compiler_params=params,
        name="mlp",
    )(x, gains, w1, w2)


def kernel(x, mem, norm_gains, xa_wq, xa_wkv, xa_wo, mlp_w1, mlp_w2, ab_w_in, pool_w, pool_scale, ssm_conv_w, ssm_conv_b, ssm_dt_bias, ssm_a_log, ssm_d, ssm_norm, ab_w_out, cd_w_in, conf_dw_w, conf_dw_b, conf_ln_g, conf_ln_b, sc_conv_w, cd_w_out):
    depth = norm_gains.shape[0]
    gains = norm_gains.astype(F32)
    wq, wo, w1, w2 = (w.astype(BF16) for w in (xa_wq, xa_wo, mlp_w1, mlp_w2))
    kv = _kv_proj(mem, xa_wkv)
    for layer in range(depth):
        g = gains[layer]
        i = layer // 2
        if layer % 2 == 0:
            mix = _mixer_ab(x, g, ab_w_in[i], pool_w[i], pool_scale[i], ssm_conv_w[i], ssm_conv_b[i],
                            ssm_dt_bias[i], ssm_a_log[i], ssm_d[i], ssm_norm[i])
            w_out = ab_w_out[i]
        else:
            mix = _mixer_cd(x, g, cd_w_in[i], conf_dw_w[i], conf_dw_b[i], conf_ln_g[i], conf_ln_b[i],
                            sc_conv_w[i])
            w_out = cd_w_out[i]
        x = _post(layer, x, mix, kv, gains, w_out, wq, wo, w1, w2)
    return x
```

```python
import jax
import jax.numpy as jnp
from jax import lax
from jax.experimental import pallas as pl
from jax.experimental.pallas import tpu as pltpu

F32 = jnp.float32
BF16 = jnp.bfloat16

D_MODEL = 1024
N_MEM = 256
XA_HEADS = 4
XA_HEAD_DIM = D_MODEL // XA_HEADS

LANES = 128
MXU_N = 256

POOL_WINDOWS = (2, 4, 8, 16)
POOL_GROUP_DIM = 128
POOL_WIDTH = len(POOL_WINDOWS) * POOL_GROUP_DIM
POOL_HALO = 16

SSM_HEAD_DIM = 64
SSM_INNER = 1024
SSM_HEADS = 16
SSM_GROUPS = 2
SSM_STATE = 128
SSM_CONV = 4
SSM_CONV_HALO = 8
SSM_CHUNK = 128
SSM_GN = SSM_GROUPS * SSM_STATE
SSM_CONV_DIM = SSM_INNER + 2 * SSM_GN
AB_IN = POOL_WIDTH + SSM_INNER + SSM_CONV_DIM + SSM_HEADS
AB_OUT = POOL_WIDTH + SSM_INNER
AB_IN_PAD = POOL_WIDTH + SSM_INNER + SSM_CONV_DIM + LANES
OFF_Z = POOL_WIDTH
OFF_XBC = OFF_Z + SSM_INNER
OFF_DT = OFF_XBC + SSM_CONV_DIM

CONF_DIM = 1024
CONF_KERNEL = 31
CONF_HALO = 32
SC_DIM = 1024
SC_KERNEL = 3
SC_HALO = 8
CD_IN = 2 * CONF_DIM + 3 * SC_DIM
CD_OUT = CONF_DIM + SC_DIM
CONV_ROWS = 64
CD_BLOCKS_PER_GROUP = CD_IN // CONF_DIM

MLP_HIDDEN = 4096
MLP_CHUNK = 512
RMS_EPS = 1e-6
LN_EPS = 1e-5
LOG2_E = 1.4426950408889634

SEQ_TILE = 512
POST_TILE = 1024
ATTN_ROW_SPLIT = 2
VMEM_LIMIT = 56 * 1024 * 1024


def _rms(x, g):
    return x * lax.rsqrt(jnp.mean(x * x, axis=-1, keepdims=True) + RMS_EPS) * g


def _sigmoid(x):
    return 0.5 * jnp.tanh(0.5 * x) + 0.5


def _silu(x):
    hx = 0.5 * x
    return hx * jnp.tanh(hx) + hx


def _dot(a, b):
    return jnp.dot(a, b, preferred_element_type=F32)


def _const_spec(shape):
    nd = len(shape)
    return pl.BlockSpec(shape, lambda *_: (0,) * nd, pipeline_mode=pl.Buffered(1))


def _layer_spec(stack_shape, layer):
    nd = len(stack_shape) - 1
    return pl.BlockSpec((None,) + tuple(stack_shape[1:]), lambda *_: (layer,) + (0,) * nd,
                        pipeline_mode=pl.Buffered(1))


def _store_slabs(slab_ref, first_slab, row0, val):
    rows = val.shape[0]
    for j in range(val.shape[1] // LANES):
        slab_ref[first_slab + j, row0:row0 + rows, :] = val[:, j * LANES:(j + 1) * LANES]


def _dwconv(slab_ref, slab, w_ref, cols, width, halo, row0, rows, init):
    acc = init
    for k in range(width):
        r = halo - (width - 1) + k + row0
        acc = acc + w_ref[k:k + 1, cols] * slab_ref[slab, r:r + rows, :]
    return acc


def _ab_kernel(x_ref, g_ref, win_ref, poolw_ref, pscale_ref, convw_ref, convb_ref, dtb_ref,
               alog_ref, dskip_ref, normw_ref, o_ref, pool_s, xc_s, z_s, state_s):
    tq = x_ref.shape[0]
    s_idx = pl.program_id(1)
    HP, HC, L = POOL_HALO, SSM_CONV_HALO, SSM_CHUNK
    n_chunks = tq // L
    x_slabs = SSM_INNER // LANES
    b_slab0 = x_slabs
    c_slab0 = x_slabs + SSM_GN // LANES

    @pl.when(s_idx == 0)
    def _():
        pool_s[:, 0:HP, :] = jnp.zeros((pool_s.shape[0], HP, LANES), F32)
        xc_s[:, 0:HC, :] = jnp.zeros((xc_s.shape[0], HC, LANES), F32)
        state_s[...] = jnp.zeros_like(state_s)

    h = _rms(x_ref[...], g_ref[0:1, :]).astype(BF16)

    def conv_silu(slab, row0):
        cols = slice(slab * LANES, (slab + 1) * LANES)
        return _silu(_dwconv(xc_s, slab, convw_ref, cols, SSM_CONV, HC, row0, L, convb_ref[:, cols]))

    _store_slabs(pool_s, 0, HP, _dot(h, win_ref[:, 0:POOL_WIDTH]))
    pos = (s_idx * tq + 1 + lax.broadcasted_iota(jnp.int32, (tq, 1), 0)).astype(F32)
    for g, w in enumerate(POOL_WINDOWS):
        c0 = g * POOL_GROUP_DIM
        cur = pool_s[g, HP:HP + tq, :]
        acc = cur
        for j in range(1, w):
            acc = acc + pool_s[g, HP - j:HP - j + tq, :]
        mean = acc / jnp.minimum(pos, float(w))
        po = _dot((mean - cur).astype(BF16), poolw_ref[g]) * pscale_ref[:, c0:c0 + POOL_GROUP_DIM]
        o_ref[:, c0:c0 + POOL_GROUP_DIM] = po.astype(o_ref.dtype)

    a_row = -jnp.exp(alog_ref[...])
    ri = lax.broadcasted_iota(jnp.int32, (L, L), 0)
    ci = lax.broadcasted_iota(jnp.int32, (L, L), 1)
    causal = ri >= ci
    tril = jnp.where(causal, 1.0, 0.0).astype(BF16)
    lo_half = lax.broadcasted_iota(jnp.int32, (1, LANES), 1) < SSM_HEAD_DIM
    dt_raw = _dot(h, win_ref[:, OFF_DT:OFF_DT + LANES])
    dtps, css, cs_ts = [], [], []
    for c in range(n_chunks):
        dtp = jax.nn.softplus(dt_raw[c * L:(c + 1) * L, :] + dtb_ref[...])
        d_a = dtp * a_row
        hi = d_a.astype(BF16)
        r1 = d_a - hi.astype(F32)
        mid = r1.astype(BF16)
        lo = (r1 - mid.astype(F32)).astype(BF16)
        cs = _dot(tril, hi) + _dot(tril, mid) + _dot(tril, lo)
        cs = cs * LOG2_E
        dtps.append(dtp)
        css.append(cs)
        cs_ts.append(cs.T)

    for c0 in range(0, SSM_CONV_DIM, MXU_N):
        _store_slabs(xc_s, c0 // LANES, HC, _dot(h, win_ref[:, OFF_XBC + c0:OFF_XBC + c0 + MXU_N]))
    for c0 in range(0, SSM_INNER, MXU_N):
        z_s[:, c0:c0 + MXU_N] = _dot(h, win_ref[:, OFF_Z + c0:OFF_Z + c0 + MXU_N])

    slabs_per_group = x_slabs // SSM_GROUPS
    group_width = SSM_INNER // SSM_GROUPS
    for c in range(n_chunks):
        rows = slice(c * L, (c + 1) * L)
        cs, cs_t, dtp = css[c], cs_ts[c], dtps[c]
        for g in range(SSM_GROUPS):
            gcols = slice(g * group_width, (g + 1) * group_width)
            bm = conv_silu(b_slab0 + g, c * L)
            cm_b = conv_silu(c_slab0 + g, c * L).astype(BF16)
            bm_t = bm.T.astype(BF16)
            cb = lax.dot_general(cm_b, bm.astype(BF16), (((1,), (1,)), ((), ())),
                                 preferred_element_type=F32)
            xs_l, xdt_l, cs_l, yd_l = [], [], [], []
            for sl in range(slabs_per_group):
                slab = g * slabs_per_group + sl
                h0 = 2 * slab
                xs = conv_silu(slab, c * L)
                col0 = jnp.broadcast_to(cs[:, h0:h0 + 1], (L, L))
                col1 = jnp.broadcast_to(cs[:, h0 + 1:h0 + 2], (L, L))
                row0 = jnp.broadcast_to(cs_t[h0:h0 + 1, :], (L, L))
                row1 = jnp.broadcast_to(cs_t[h0 + 1:h0 + 2, :], (L, L))
                dt_pair = jnp.where(lo_half, jnp.broadcast_to(dtp[:, h0:h0 + 1], (L, LANES)),
                                    jnp.broadcast_to(dtp[:, h0 + 1:h0 + 2], (L, LANES)))
                xdt = xs * dt_pair
                xdt_b = xdt.astype(BF16)
                zero_b = jnp.zeros_like(xdt_b)
                m0 = (cb * jnp.exp2(jnp.where(causal, col0 - row0, -jnp.inf))).astype(BF16)
                m1 = (cb * jnp.exp2(jnp.where(causal, col1 - row1, -jnp.inf))).astype(BF16)
                m01 = jnp.concatenate([m0, m1], axis=1)
                x01 = jnp.concatenate([jnp.where(lo_half, xdt_b, zero_b),
                                       jnp.where(lo_half, zero_b, xdt_b)], axis=0)
                yd_l.append(_dot(m01, x01))
                xs_l.append(xs)
                xdt_l.append(xdt)
                cs_l.append(jnp.where(lo_half, col0, col1))
            xs_g = jnp.concatenate(xs_l, axis=1)
            xdt_g = jnp.concatenate(xdt_l, axis=1)
            cs_g = jnp.concatenate(cs_l, axis=1)
            hs = state_s[:, gcols]
            y = jnp.concatenate(yd_l, axis=1) + _dot(cm_b, hs.astype(BF16)) * jnp.exp2(cs_g)
            last = cs_g[L - 1:L, :]
            xw = (xdt_g * jnp.exp2(last - cs_g)).astype(BF16)
            state_s[:, gcols] = hs * jnp.exp2(last) + _dot(bm_t, xw)
            y = y + dskip_ref[:, gcols] * xs_g
            y = y * _silu(z_s[rows, gcols])
            scale = lax.rsqrt(jnp.mean(y * y, axis=-1, keepdims=True) + RMS_EPS)
            yo = y * scale * normw_ref[:, gcols]
            o_ref[rows, POOL_WIDTH + g * group_width:POOL_WIDTH + (g + 1) * group_width] = yo.astype(o_ref.dtype)

    pool_s[:, 0:HP, :] = pool_s[:, tq:tq + HP, :]
    xc_s[:, 0:HC, :] = xc_s[:, tq:tq + HC, :]


def _mixer_ab(x, gains, w_in, pool_w, pool_scale, conv_w, conv_b, dt_bias, a_log, d_skip, norm_w):
    B, S, D = x.shape
    tq = min(SEQ_TILE, S)
    pad = AB_IN_PAD - AB_IN
    win = jnp.pad(w_in, ((0, 0), (0, pad))).astype(BF16)
    row = lambda v: v.reshape(1, -1).astype(F32)
    lane_pad = lambda v: jnp.pad(v.astype(F32), (0, LANES - v.shape[0])).reshape(1, LANES)
    args = (
        gains, win, pool_w.astype(BF16), row(pool_scale), conv_w.astype(F32), row(conv_b),
        lane_pad(dt_bias), lane_pad(a_log), row(jnp.repeat(d_skip, SSM_HEAD_DIM)), row(norm_w),
    )
    return pl.pallas_call(
        _ab_kernel,
        grid=(B, S // tq),
        in_specs=[pl.BlockSpec((None, tq, D), lambda b, s: (b, s, 0))] + [_const_spec(a.shape) for a in args],
        out_specs=pl.BlockSpec((None, tq, AB_OUT), lambda b, s: (b, s, 0)),
        out_shape=jax.ShapeDtypeStruct((B, S, AB_OUT), BF16),
        scratch_shapes=[
            pltpu.VMEM((POOL_WIDTH // LANES, POOL_HALO + tq, LANES), F32),
            pltpu.VMEM((SSM_CONV_DIM // LANES, SSM_CONV_HALO + tq, LANES), F32),
            pltpu.VMEM((tq, SSM_INNER), F32),
            pltpu.VMEM((SSM_STATE, SSM_INNER), F32),
        ],
        compiler_params=pltpu.CompilerParams(
            dimension_semantics=("arbitrary", "arbitrary"), vmem_limit_bytes=VMEM_LIMIT),
        name="mixer_ab",
    )(x, *args)


def _cd_kernel(x_ref, g_ref, win_ref, dww_ref, dwb_ref, lng_ref, lnb_ref, scw_ref, o_ref, *scratch):
    tq = x_ref.shape[0]
    s_idx = pl.program_id(1)
    HV, HP = CONF_HALO, SC_HALO
    slabs_per_col_group = MXU_N // LANES
    n_groups = CONF_DIM // MXU_N
    v_grp = scratch[0:n_groups]
    p_grp = scratch[n_groups:2 * n_groups]
    c_s, h_s = scratch[2 * n_groups:]

    @pl.when(s_idx == 0)
    def _():
        for v_s in v_grp:
            v_s[:, 0:HV, :] = jnp.zeros((slabs_per_col_group, HV, LANES), F32)
        for p_s in p_grp:
            p_s[:, 0:HP, :] = jnp.zeros((slabs_per_col_group, HP, LANES), F32)


    def project(cg, blk):
        base = (cg * CD_BLOCKS_PER_GROUP + blk) * MXU_N
        return _dot(h_s[...], win_ref[:, base:base + MXU_N])

    def project_glu(cg):
        for j in range(slabs_per_col_group):
            r = project(cg, j)
            v_grp[cg][j, HV:HV + tq, :] = r[:, 0:LANES] * _sigmoid(r[:, LANES:])

    def short_conv(cg):
        for j in range(slabs_per_col_group):
            r = project(cg, slabs_per_col_group + j)
            p_grp[cg][j, HP:HP + tq, :] = r[:, 0:LANES] * r[:, LANES:]
        bg = project(cg, 2 * slabs_per_col_group)
        for j in range(slabs_per_col_group):
            slab = cg * slabs_per_col_group + j
            cols = slice(slab * LANES, (slab + 1) * LANES)
            conv = _dwconv(p_grp[cg], j, scw_ref, cols, SC_KERNEL, HP, 0, tq, jnp.zeros((1, LANES), F32))
            o_ref[:, CONF_DIM + slab * LANES:CONF_DIM + (slab + 1) * LANES] = (
                bg[:, j * LANES:(j + 1) * LANES] * conv).astype(o_ref.dtype)

    def conv31_block(cg, blk):
        j, rb = divmod(blk, tq // CONV_ROWS)
        slab = cg * slabs_per_col_group + j
        cols = slice(slab * LANES, (slab + 1) * LANES)
        r0 = rb * CONV_ROWS
        c_s[r0:r0 + CONV_ROWS, cols] = _dwconv(v_grp[cg], j, dww_ref, cols, CONF_KERNEL, HV, r0,
                                               CONV_ROWS, dwb_ref[:, cols])

    blocks_per_group = slabs_per_col_group * (tq // CONV_ROWS)
    third = blocks_per_group // 3

    h_s[...] = _rms(x_ref[...], g_ref[0:1, :]).astype(BF16)
    project_glu(0)
    for cg in range(n_groups):
        for blk in range(0, third):
            conv31_block(cg, blk)
        if cg + 1 < n_groups:
            project_glu(cg + 1)
        for blk in range(third, 2 * third):
            conv31_block(cg, blk)
        short_conv(cg)
        for blk in range(2 * third, blocks_per_group):
            conv31_block(cg, blk)

    for rb in range(tq // CONV_ROWS):
        rows = slice(rb * CONV_ROWS, (rb + 1) * CONV_ROWS)
        acc = c_s[rows, :]
        mu = jnp.mean(acc, axis=-1, keepdims=True)
        cen = acc - mu
        var = jnp.mean(cen * cen, axis=-1, keepdims=True)
        vn = cen * lax.rsqrt(var + LN_EPS) * lng_ref[...] + lnb_ref[...]
        o_ref[rows, 0:CONF_DIM] = _silu(vn).astype(o_ref.dtype)

    for v_s in v_grp:
        v_s[:, 0:HV, :] = v_s[:, tq:tq + HV, :]
    for p_s in p_grp:
        p_s[:, 0:HP, :] = p_s[:, tq:tq + HP, :]


def _cd_weight_layout(w_in):
    val, gate, bg, cgate, hh = range(5)

    def tile(part, s):
        c0 = part * CONF_DIM + s * LANES
        return w_in[:, c0:c0 + LANES]

    cols = []
    for cg in range(CONF_DIM // MXU_N):
        a, b = 2 * cg, 2 * cg + 1
        cols += [tile(val, a), tile(gate, a), tile(val, b), tile(gate, b),
                 tile(cgate, a), tile(hh, a), tile(cgate, b), tile(hh, b), tile(bg, a), tile(bg, b)]
    return jnp.concatenate(cols, axis=1)


def _mixer_cd(x, gains, w_in, dw_w, dw_b, ln_g, ln_b, sc_w):
    B, S, D = x.shape
    tq = min(SEQ_TILE, S)
    row = lambda v: v.reshape(1, -1).astype(F32)
    args = (gains, _cd_weight_layout(w_in).astype(BF16), dw_w.astype(F32), row(dw_b), row(ln_g), row(ln_b),
            sc_w.astype(F32))
    return pl.pallas_call(
        _cd_kernel,
        grid=(B, S // tq),
        in_specs=[pl.BlockSpec((None, tq, D), lambda b, s: (b, s, 0))] + [_const_spec(a.shape) for a in args],
        out_specs=pl.BlockSpec((None, tq, CD_OUT), lambda b, s: (b, s, 0)),
        out_shape=jax.ShapeDtypeStruct((B, S, CD_OUT), BF16),
        scratch_shapes=(
            [pltpu.VMEM((MXU_N // LANES, CONF_HALO + tq, LANES), F32)] * (CONF_DIM // MXU_N)
            + [pltpu.VMEM((MXU_N // LANES, SC_HALO + tq, LANES), F32)] * (SC_DIM // MXU_N)
            + [pltpu.VMEM((tq, CONF_DIM), F32), pltpu.VMEM((tq, D), BF16)]
        ),
        compiler_params=pltpu.CompilerParams(
            dimension_semantics=("arbitrary", "arbitrary"), vmem_limit_bytes=VMEM_LIMIT),
        name="mixer_cd",
    )(x, *args)


def _kv_kernel(mem_ref, wkv_ref, o_ref):
    o_ref[...] = _dot(mem_ref[...].astype(BF16), wkv_ref[...]).astype(o_ref.dtype)


def _kv_proj(mem, wkv):
    B, M, D = mem.shape
    depth = wkv.shape[0]
    rows = B * M
    tm = min(SEQ_TILE, rows)
    kv = pl.pallas_call(
        _kv_kernel,
        grid=(depth, rows // tm),
        in_specs=[pl.BlockSpec((tm, D), lambda l, r: (r, 0)),
                  pl.BlockSpec((None, D, 2 * D), lambda l, r: (l, 0, 0))],
        out_specs=pl.BlockSpec((None, tm, 2 * D), lambda l, r: (l, r, 0)),
        out_shape=jax.ShapeDtypeStruct((depth, rows, 2 * D), BF16),
        compiler_params=pltpu.CompilerParams(
            dimension_semantics=("arbitrary", "arbitrary"), vmem_limit_bytes=VMEM_LIMIT),
        name="kv_proj",
    )(mem.reshape(rows, D), wkv.astype(BF16))
    return kv.reshape(depth, B, M, 2 * D)


def _attn_kernel(x_ref, mix_ref, kv_ref, g_ref, wout_ref, wq_ref, wo_ref, o_ref):
    tm = x_ref.shape[0]
    half = tm // ATTN_ROW_SPLIT
    halves = [slice(i * half, (i + 1) * half) for i in range(ATTN_ROW_SPLIT)]

    xs = [x_ref[r, :] + _rms(_dot(mix_ref[r, :], wout_ref[...]), g_ref[1:2, :]) for r in halves]

    qs = [_dot(_rms(x, g_ref[2:3, :]).astype(BF16), wq_ref[...]) for x in xs]
    outs = []
    for q in qs:
        heads = []
        for hd in range(XA_HEADS):
            cols = slice(hd * XA_HEAD_DIM, (hd + 1) * XA_HEAD_DIM)
            kh = kv_ref[:, cols]
            vh = kv_ref[:, D_MODEL + hd * XA_HEAD_DIM:D_MODEL + (hd + 1) * XA_HEAD_DIM]
            s = lax.dot_general(q[:, cols].astype(BF16), kh, (((1,), (1,)), ((), ())),
                                preferred_element_type=F32) * (1.0 / XA_HEAD_DIM ** 0.5)
            e = jnp.exp(s - jnp.max(s, axis=-1, keepdims=True))
            p = e * (1.0 / jnp.sum(e, axis=-1, keepdims=True))
            heads.append(_dot(p.astype(BF16), vh).astype(BF16))
        outs.append(jnp.concatenate(heads, axis=-1))
    for r, x, o in zip(halves, xs, outs):
        o_ref[r, :] = x + _rms(_dot(o, wo_ref[...]), g_ref[3:4, :])


def _mlp_kernel(x_ref, g_ref, w1_ref, w2_ref, o_ref):
    tm = x_ref.shape[0]
    x = x_ref[...]
    h = _rms(x, g_ref[4:5, :]).astype(BF16)
    acc = jnp.zeros((tm, D_MODEL), F32)
    for c in range(0, MLP_HIDDEN, MLP_CHUNK):
        r = jnp.maximum(_dot(h, w1_ref[:, c:c + MLP_CHUNK]), 0.0)
        acc = acc + _dot((r * r).astype(BF16), w2_ref[c:c + MLP_CHUNK, :])
    o_ref[...] = x + _rms(acc, g_ref[5:6, :])


def _post(layer, x, mix, kv, gains, w_out, wq, wo, w1, w2):
    B, S, D = x.shape
    tm = min(POST_TILE, S)
    w_out = w_out.astype(BF16)
    tile = lambda width: pl.BlockSpec((None, tm, width), lambda b, s: (b, s, 0))
    params = pltpu.CompilerParams(dimension_semantics=("arbitrary", "arbitrary"), vmem_limit_bytes=VMEM_LIMIT)
    x = pl.pallas_call(
        _attn_kernel,
        grid=(B, S // tm),
        in_specs=[
            tile(D), tile(mix.shape[-1]),
            pl.BlockSpec((None, None, kv.shape[2], kv.shape[3]), lambda b, s: (layer, b, 0, 0)),
            _layer_spec(gains.shape, layer), _const_spec(w_out.shape),
            _layer_spec(wq.shape, layer), _layer_spec(wo.shape, layer),
        ],
        out_specs=tile(D),
        out_shape=jax.ShapeDtypeStruct((B, S, D), F32),
        compiler_params=params,
        name="attn",
    )(x, mix, kv, gains, w_out, wq, wo)
    return pl.pallas_call(
        _mlp_kernel,
        grid=(B, S // tm),
        in_specs=[tile(D), _layer_spec(gains.shape, layer), _layer_spec(w1.shape, layer),
                  _layer_spec(w2.shape, layer)],
        out_specs=tile(D),
        out_shape=jax.ShapeDtypeStruct((B, S, D), F32),
        compiler_params=params,
        name="mlp",
    )(x, gains, w1, w2)


def kernel(x, mem, norm_gains, xa_wq, xa_wkv, xa_wo, mlp_w1, mlp_w2, ab_w_in, pool_w, pool_scale, ssm_conv_w, ssm_conv_b, ssm_dt_bias, ssm_a_log, ssm_d, ssm_norm, ab_w_out, cd_w_in, conf_dw_w, conf_dw_b, conf_ln_g, conf_ln_b, sc_conv_w, cd_w_out):
    depth = norm_gains.shape[0]
    gains = norm_gains.astype(F32)
    wq, wo, w1, w2 = (w.astype(BF16) for w in (xa_wq, xa_wo, mlp_w1, mlp_w2))
    kv = _kv_proj(mem, xa_wkv)
    for layer in range(depth):
        g = gains[layer]
        i = layer // 2
        if layer % 2 == 0:
            mix = _mixer_ab(x, g, ab_w_in[i], pool_w[i], pool_scale[i], ssm_conv_w[i], ssm_conv_b[i],
                            ssm_dt_bias[i], ssm_a_log[i], ssm_d[i], ssm_norm[i])
            w_out = ab_w_out[i]
        else:
            mix = _mixer_cd(x, g, cd_w_in[i], conf_dw_w[i], conf_dw_b[i], conf_ln_g[i], conf_ln_b[i],
                            sc_conv_w[i])
            w_out = cd_w_out[i]
        x = _post(layer, x, mix, kv, gains, w_out, wq, wo, w1, w2)
    return x
```

```python
import jax
import jax.numpy as jnp
from jax import lax
from jax.experimental import pallas as pl
from jax.experimental.pallas import tpu as pltpu

F32 = jnp.float32
BF16 = jnp.bfloat16

D_MODEL = 1024
N_MEM = 256
XA_HEADS = 4
XA_HEAD_DIM = D_MODEL // XA_HEADS

LANES = 128
MXU_N = 256

POOL_WINDOWS = (2, 4, 8, 16)
POOL_GROUP_DIM = 128
POOL_WIDTH = len(POOL_WINDOWS) * POOL_GROUP_DIM
POOL_HALO = 16

SSM_HEAD_DIM = 64
SSM_INNER = 1024
SSM_HEADS = 16
SSM_GROUPS = 2
SSM_STATE = 128
SSM_CONV = 4
SSM_CONV_HALO = 8
SSM_CHUNK = 128
SSM_GN = SSM_GROUPS * SSM_STATE
SSM_CONV_DIM = SSM_INNER + 2 * SSM_GN
AB_IN = POOL_WIDTH + SSM_INNER + SSM_CONV_DIM + SSM_HEADS
AB_OUT = POOL_WIDTH + SSM_INNER
AB_IN_PAD = POOL_WIDTH + SSM_INNER + SSM_CONV_DIM + LANES
OFF_Z = POOL_WIDTH
OFF_XBC = OFF_Z + SSM_INNER
OFF_DT = OFF_XBC + SSM_CONV_DIM

CONF_DIM = 1024
CONF_KERNEL = 31
CONF_HALO = 32
SC_DIM = 1024
SC_KERNEL = 3
SC_HALO = 8
CD_IN = 2 * CONF_DIM + 3 * SC_DIM
CD_OUT = CONF_DIM + SC_DIM
CONV_ROWS = 64
CD_BLOCKS_PER_GROUP = CD_IN // CONF_DIM

MLP_HIDDEN = 4096
MLP_CHUNK = 512
RMS_EPS = 1e-6
LN_EPS = 1e-5
LOG2_E = 1.4426950408889634

SEQ_TILE = 512
POST_TILE = 1024
ATTN_ROW_SPLIT = 2
VMEM_LIMIT = 56 * 1024 * 1024


def _rms(x, g):
    return x * lax.rsqrt(jnp.mean(x * x, axis=-1, keepdims=True) + RMS_EPS) * g


def _sigmoid(x):
    return 0.5 * jnp.tanh(0.5 * x) + 0.5


def _silu(x):
    hx = 0.5 * x
    return hx * jnp.tanh(hx) + hx


def _dot(a, b):
    return jnp.dot(a, b, preferred_element_type=F32)


def _const_spec(shape):
    nd = len(shape)
    return pl.BlockSpec(shape, lambda *_: (0,) * nd, pipeline_mode=pl.Buffered(1))


def _layer_spec(stack_shape, layer):
    nd = len(stack_shape) - 1
    return pl.BlockSpec((None,) + tuple(stack_shape[1:]), lambda *_: (layer,) + (0,) * nd,
                        pipeline_mode=pl.Buffered(1))


def _store_slabs(slab_ref, first_slab, row0, val):
    rows = val.shape[0]
    for j in range(val.shape[1] // LANES):
        slab_ref[first_slab + j, row0:row0 + rows, :] = val[:, j * LANES:(j + 1) * LANES]


def _dwconv(slab_ref, slab, w_ref, cols, width, halo, row0, rows, init):
    acc = init
    for k in range(width):
        r = halo - (width - 1) + k + row0
        acc = acc + w_ref[k:k + 1, cols] * slab_ref[slab, r:r + rows, :]
    return acc


def _ab_kernel(x_ref, g_ref, win_ref, poolw_ref, pscale_ref, convw_ref, convb_ref, dtb_ref,
               alog_ref, dskip_ref, normw_ref, o_ref, pool_s, xc_s, z_s, state_s):
    tq = x_ref.shape[0]
    s_idx = pl.program_id(1)
    HP, HC, L = POOL_HALO, SSM_CONV_HALO, SSM_CHUNK
    n_chunks = tq // L
    x_slabs = SSM_INNER // LANES
    b_slab0 = x_slabs
    c_slab0 = x_slabs + SSM_GN // LANES

    @pl.when(s_idx == 0)
    def _():
        pool_s[:, 0:HP, :] = jnp.zeros((pool_s.shape[0], HP, LANES), F32)
        xc_s[:, 0:HC, :] = jnp.zeros((xc_s.shape[0], HC, LANES), F32)
        state_s[...] = jnp.zeros_like(state_s)

    h = _rms(x_ref[...], g_ref[0:1, :]).astype(BF16)

    def conv_silu(slab, row0):
        cols = slice(slab * LANES, (slab + 1) * LANES)
        return _silu(_dwconv(xc_s, slab, convw_ref, cols, SSM_CONV, HC, row0, L, convb_ref[:, cols]))

    _store_slabs(pool_s, 0, HP, _dot(h, win_ref[:, 0:POOL_WIDTH]))
    pos = (s_idx * tq + 1 + lax.broadcasted_iota(jnp.int32, (tq, 1), 0)).astype(F32)
    for g, w in enumerate(POOL_WINDOWS):
        c0 = g * POOL_GROUP_DIM
        cur = pool_s[g, HP:HP + tq, :]
        acc = cur
        for j in range(1, w):
            acc = acc + pool_s[g, HP - j:HP - j + tq, :]
        mean = acc / jnp.minimum(pos, float(w))
        po = _dot((mean - cur).astype(BF16), poolw_ref[g]) * pscale_ref[:, c0:c0 + POOL_GROUP_DIM]
        o_ref[:, c0:c0 + POOL_GROUP_DIM] = po.astype(o_ref.dtype)

    a_row = -jnp.exp(alog_ref[...])
    ri = lax.broadcasted_iota(jnp.int32, (L, L), 0)
    ci = lax.broadcasted_iota(jnp.int32, (L, L), 1)
    causal = ri >= ci
    tril = jnp.where(causal, 1.0, 0.0).astype(BF16)
    lo_half = lax.broadcasted_iota(jnp.int32, (1, LANES), 1) < SSM_HEAD_DIM
    dt_raw = _dot(h, win_ref[:, OFF_DT:OFF_DT + LANES])
    dtps, css, cs_ts = [], [], []
    for c in range(n_chunks):
        dtp = jax.nn.softplus(dt_raw[c * L:(c + 1) * L, :] + dtb_ref[...])
        d_a = dtp * a_row
        hi = d_a.astype(BF16)
        r1 = d_a - hi.astype(F32)
        mid = r1.astype(BF16)
        lo = (r1 - mid.astype(F32)).astype(BF16)
        cs = _dot(tril, hi) + _dot(tril, mid) + _dot(tril, lo)
        cs = cs * LOG2_E
        dtps.append(dtp)
        css.append(cs)
        cs_ts.append(cs.T)

    for c0 in range(0, SSM_CONV_DIM, MXU_N):
        _store_slabs(xc_s, c0 // LANES, HC, _dot(h, win_ref[:, OFF_XBC + c0:OFF_XBC + c0 + MXU_N]))
    for c0 in range(0, SSM_INNER, MXU_N):
        z_s[:, c0:c0 + MXU_N] = _dot(h, win_ref[:, OFF_Z + c0:OFF_Z + c0 + MXU_N])

    slabs_per_group = x_slabs // SSM_GROUPS
    group_width = SSM_INNER // SSM_GROUPS
    for c in range(n_chunks):
        rows = slice(c * L, (c + 1) * L)
        cs, cs_t, dtp = css[c], cs_ts[c], dtps[c]
        for g in range(SSM_GROUPS):
            gcols = slice(g * group_width, (g + 1) * group_width)
            bm = conv_silu(b_slab0 + g, c * L)
            cm_b = conv_silu(c_slab0 + g, c * L).astype(BF16)
            bm_t = bm.T.astype(BF16)
            cb = lax.dot_general(cm_b, bm.astype(BF16), (((1,), (1,)), ((), ())),
                                 preferred_element_type=F32)
            xs_l, xdt_l, cs_l, yd_l = [], [], [], []
            for sl in range(slabs_per_group):
                slab = g * slabs_per_group + sl
                h0 = 2 * slab
                xs = conv_silu(slab, c * L)
                col0 = jnp.broadcast_to(cs[:, h0:h0 + 1], (L, L))
                col1 = jnp.broadcast_to(cs[:, h0 + 1:h0 + 2], (L, L))
                row0 = jnp.broadcast_to(cs_t[h0:h0 + 1, :], (L, L))
                row1 = jnp.broadcast_to(cs_t[h0 + 1:h0 + 2, :], (L, L))
                dt_pair = jnp.where(lo_half, jnp.broadcast_to(dtp[:, h0:h0 + 1], (L, LANES)),
                                    jnp.broadcast_to(dtp[:, h0 + 1:h0 + 2], (L, LANES)))
                xdt = xs * dt_pair
                xdt_b = xdt.astype(BF16)
                zero_b = jnp.zeros_like(xdt_b)
                m0 = (cb * jnp.exp2(jnp.where(causal, col0 - row0, -jnp.inf))).astype(BF16)
                m1 = (cb * jnp.exp2(jnp.where(causal, col1 - row1, -jnp.inf))).astype(BF16)
                m01 = jnp.concatenate([m0, m1], axis=1)
                x01 = jnp.concatenate([jnp.where(lo_half, xdt_b, zero_b),
                                       jnp.where(lo_half, zero_b, xdt_b)], axis=0)
                yd_l.append(_dot(m01, x01))
                xs_l.append(xs)
                xdt_l.append(xdt)
                cs_l.append(jnp.where(lo_half, col0, col1))
            xs_g = jnp.concatenate(xs_l, axis=1)
            xdt_g = jnp.concatenate(xdt_l, axis=1)
            cs_g = jnp.concatenate(cs_l, axis=1)
            hs = state_s[:, gcols]
            y = jnp.concatenate(yd_l, axis=1) + _dot(cm_b, hs.astype(BF16)) * jnp.exp2(cs_g)
            last = cs_g[L - 1:L, :]
            xw = (xdt_g * jnp.exp2(last - cs_g)).astype(BF16)
            state_s[:, gcols] = hs * jnp.exp2(last) + _dot(bm_t, xw)
            y = y + dskip_ref[:, gcols] * xs_g
            y = y * _silu(z_s[rows, gcols])
            scale = lax.rsqrt(jnp.mean(y * y, axis=-1, keepdims=True) + RMS_EPS)
            yo = y * scale * normw_ref[:, gcols]
            o_ref[rows, POOL_WIDTH + g * group_width:POOL_WIDTH + (g + 1) * group_width] = yo.astype(o_ref.dtype)

    pool_s[:, 0:HP, :] = pool_s[:, tq:tq + HP, :]
    xc_s[:, 0:HC, :] = xc_s[:, tq:tq + HC, :]


def _mixer_ab(x, gains, w_in, pool_w, pool_scale, conv_w, conv_b, dt_bias, a_log, d_skip, norm_w):
    B, S, D = x.shape
    tq = min(SEQ_TILE, S)
    pad = AB_IN_PAD - AB_IN
    win = jnp.pad(w_in.astype(BF16), ((0, 0), (0, pad)))
    row = lambda v: v.reshape(1, -1).astype(F32)
    lane_pad = lambda v: jnp.pad(v.astype(F32), (0, LANES - v.shape[0])).reshape(1, LANES)
    args = (
        gains, win, pool_w.astype(BF16), row(pool_scale), conv_w.astype(F32), row(conv_b),
        lane_pad(dt_bias), lane_pad(a_log), row(jnp.repeat(d_skip, SSM_HEAD_DIM)), row(norm_w),
    )
    return pl.pallas_call(
        _ab_kernel,
        grid=(B, S // tq),
        in_specs=[pl.BlockSpec((None, tq, D), lambda b, s: (b, s, 0))] + [_const_spec(a.shape) for a in args],
        out_specs=pl.BlockSpec((None, tq, AB_OUT), lambda b, s: (b, s, 0)),
        out_shape=jax.ShapeDtypeStruct((B, S, AB_OUT), BF16),
        scratch_shapes=[
            pltpu.VMEM((POOL_WIDTH // LANES, POOL_HALO + tq, LANES), F32),
            pltpu.VMEM((SSM_CONV_DIM // LANES, SSM_CONV_HALO + tq, LANES), F32),
            pltpu.VMEM((tq, SSM_INNER), F32),
            pltpu.VMEM((SSM_STATE, SSM_INNER), F32),
        ],
        compiler_params=pltpu.CompilerParams(
            dimension_semantics=("arbitrary", "arbitrary"), vmem_limit_bytes=VMEM_LIMIT),
        name="mixer_ab",
    )(x, *args)


def _cd_kernel(x_ref, g_ref, win_ref, dww_ref, dwb_ref, lng_ref, lnb_ref, scw_ref, o_ref, *scratch):
    tq = x_ref.shape[0]
    s_idx = pl.program_id(1)
    HV, HP = CONF_HALO, SC_HALO
    slabs_per_col_group = MXU_N // LANES
    n_groups = CONF_DIM // MXU_N
    v_grp = scratch[0:n_groups]
    p_grp = scratch[n_groups:2 * n_groups]
    c_s, h_s = scratch[2 * n_groups:]

    @pl.when(s_idx == 0)
    def _():
        for v_s in v_grp:
            v_s[:, 0:HV, :] = jnp.zeros((slabs_per_col_group, HV, LANES), F32)
        for p_s in p_grp:
            p_s[:, 0:HP, :] = jnp.zeros((slabs_per_col_group, HP, LANES), F32)


    def project(cg, blk):
        base = (cg * CD_BLOCKS_PER_GROUP + blk) * MXU_N
        return _dot(h_s[...], win_ref[:, base:base + MXU_N])

    def project_glu(cg):
        for j in range(slabs_per_col_group):
            r = project(cg, j)
            v_grp[cg][j, HV:HV + tq, :] = r[:, 0:LANES] * _sigmoid(r[:, LANES:])

    def short_conv(cg):
        for j in range(slabs_per_col_group):
            r = project(cg, slabs_per_col_group + j)
            p_grp[cg][j, HP:HP + tq, :] = r[:, 0:LANES] * r[:, LANES:]
        bg = project(cg, 2 * slabs_per_col_group)
        for j in range(slabs_per_col_group):
            slab = cg * slabs_per_col_group + j
            cols = slice(slab * LANES, (slab + 1) * LANES)
            conv = _dwconv(p_grp[cg], j, scw_ref, cols, SC_KERNEL, HP, 0, tq, jnp.zeros((1, LANES), F32))
            o_ref[:, CONF_DIM + slab * LANES:CONF_DIM + (slab + 1) * LANES] = (
                bg[:, j * LANES:(j + 1) * LANES] * conv).astype(o_ref.dtype)

    def conv31_block(cg, blk):
        j, rb = divmod(blk, tq // CONV_ROWS)
        slab = cg * slabs_per_col_group + j
        cols = slice(slab * LANES, (slab + 1) * LANES)
        r0 = rb * CONV_ROWS
        c_s[r0:r0 + CONV_ROWS, cols] = _dwconv(v_grp[cg], j, dww_ref, cols, CONF_KERNEL, HV, r0,
                                               CONV_ROWS, dwb_ref[:, cols])

    blocks_per_group = slabs_per_col_group * (tq // CONV_ROWS)
    third = blocks_per_group // 3

    h_s[...] = _rms(x_ref[...], g_ref[0:1, :]).astype(BF16)
    project_glu(0)
    for cg in range(n_groups):
        for blk in range(0, third):
            conv31_block(cg, blk)
        if cg + 1 < n_groups:
            project_glu(cg + 1)
        for blk in range(third, 2 * third):
            conv31_block(cg, blk)
        short_conv(cg)
        for blk in range(2 * third, blocks_per_group):
            conv31_block(cg, blk)

    for rb in range(tq // CONV_ROWS):
        rows = slice(rb * CONV_ROWS, (rb + 1) * CONV_ROWS)
        acc = c_s[rows, :]
        mu = jnp.mean(acc, axis=-1, keepdims=True)
        cen = acc - mu
        var = jnp.mean(cen * cen, axis=-1, keepdims=True)
        vn = cen * lax.rsqrt(var + LN_EPS) * lng_ref[...] + lnb_ref[...]
        o_ref[rows, 0:CONF_DIM] = _silu(vn).astype(o_ref.dtype)

    for v_s in v_grp:
        v_s[:, 0:HV, :] = v_s[:, tq:tq + HV, :]
    for p_s in p_grp:
        p_s[:, 0:HP, :] = p_s[:, tq:tq + HP, :]


def _cd_weight_layout(w_in):
    val, gate, bg, cgate, hh = range(5)

    def tile(part, s):
        c0 = part * CONF_DIM + s * LANES
        return w_in[:, c0:c0 + LANES]

    cols = []
    for cg in range(CONF_DIM // MXU_N):
        a, b = 2 * cg, 2 * cg + 1
        cols += [tile(val, a), tile(gate, a), tile(val, b), tile(gate, b),
                 tile(cgate, a), tile(hh, a), tile(cgate, b), tile(hh, b), tile(bg, a), tile(bg, b)]
    return jnp.concatenate(cols, axis=1)


def _mixer_cd(x, gains, w_in, dw_w, dw_b, ln_g, ln_b, sc_w):
    B, S, D = x.shape
    tq = min(SEQ_TILE, S)
    row = lambda v: v.reshape(1, -1).astype(F32)
    args = (gains, _cd_weight_layout(w_in.astype(BF16)), dw_w.astype(F32), row(dw_b), row(ln_g), row(ln_b),
            sc_w.astype(F32))
    return pl.pallas_call(
        _cd_kernel,
        grid=(B, S // tq),
        in_specs=[pl.BlockSpec((None, tq, D), lambda b, s: (b, s, 0))] + [_const_spec(a.shape) for a in args],
        out_specs=pl.BlockSpec((None, tq, CD_OUT), lambda b, s: (b, s, 0)),
        out_shape=jax.ShapeDtypeStruct((B, S, CD_OUT), BF16),
        scratch_shapes=(
            [pltpu.VMEM((MXU_N // LANES, CONF_HALO + tq, LANES), F32)] * (CONF_DIM // MXU_N)
            + [pltpu.VMEM((MXU_N // LANES, SC_HALO + tq, LANES), F32)] * (SC_DIM // MXU_N)
            + [pltpu.VMEM((tq, CONF_DIM), F32), pltpu.VMEM((tq, D), BF16)]
        ),
        compiler_params=pltpu.CompilerParams(
            dimension_semantics=("arbitrary", "arbitrary"), vmem_limit_bytes=VMEM_LIMIT),
        name="mixer_cd",
    )(x, *args)


def _kv_kernel(mem_ref, wkv_ref, o_ref):
    o_ref[...] = _dot(mem_ref[...].astype(BF16), wkv_ref[...]).astype(o_ref.dtype)


def _kv_proj(mem, wkv):
    B, M, D = mem.shape
    depth = wkv.shape[0]
    rows = B * M
    tm = min(SEQ_TILE, rows)
    kv = pl.pallas_call(
        _kv_kernel,
        grid=(depth, rows // tm),
        in_specs=[pl.BlockSpec((tm, D), lambda l, r: (r, 0)),
                  pl.BlockSpec((None, D, 2 * D), lambda l, r: (l, 0, 0))],
        out_specs=pl.BlockSpec((None, tm, 2 * D), lambda l, r: (l, r, 0)),
        out_shape=jax.ShapeDtypeStruct((depth, rows, 2 * D), BF16),
        compiler_params=pltpu.CompilerParams(
            dimension_semantics=("arbitrary", "arbitrary"), vmem_limit_bytes=VMEM_LIMIT),
        name="kv_proj",
    )(mem.reshape(rows, D), wkv.astype(BF16))
    return kv.reshape(depth, B, M, 2 * D)


def _attn_kernel(x_ref, mix_ref, kv_ref, g_ref, wout_ref, wq_ref, wo_ref, o_ref):
    tm = x_ref.shape[0]
    half = tm // ATTN_ROW_SPLIT
    halves = [slice(i * half, (i + 1) * half) for i in range(ATTN_ROW_SPLIT)]

    xs = [x_ref[r, :] + _rms(_dot(mix_ref[r, :], wout_ref[...]), g_ref[1:2, :]) for r in halves]

    qs = [_dot(_rms(x, g_ref[2:3, :]).astype(BF16), wq_ref[...]) for x in xs]
    outs = []
    for q in qs:
        heads = []
        for hd in range(XA_HEADS):
            cols = slice(hd * XA_HEAD_DIM, (hd + 1) * XA_HEAD_DIM)
            kh = kv_ref[:, cols]
            vh = kv_ref[:, D_MODEL + hd * XA_HEAD_DIM:D_MODEL + (hd + 1) * XA_HEAD_DIM]
            s = lax.dot_general(q[:, cols].astype(BF16), kh, (((1,), (1,)), ((), ())),
                                preferred_element_type=F32) * (1.0 / XA_HEAD_DIM ** 0.5)
            e = jnp.exp(s - jnp.max(s, axis=-1, keepdims=True))
            p = e * (1.0 / jnp.sum(e, axis=-1, keepdims=True))
            heads.append(_dot(p.astype(BF16), vh).astype(BF16))
        outs.append(jnp.concatenate(heads, axis=-1))
    for r, x, o in zip(halves, xs, outs):
        o_ref[r, :] = x + _rms(_dot(o, wo_ref[...]), g_ref[3:4, :])


def _mlp_kernel(x_ref, g_ref, w1_ref, w2_ref, o_ref):
    tm = x_ref.shape[0]
    x = x_ref[...]
    h = _rms(x, g_ref[4:5, :]).astype(BF16)
    acc = jnp.zeros((tm, D_MODEL), F32)
    for c in range(0, MLP_HIDDEN, MLP_CHUNK):
        r = jnp.maximum(_dot(h, w1_ref[:, c:c + MLP_CHUNK]), 0.0)
        acc = acc + _dot((r * r).astype(BF16), w2_ref[c:c + MLP_CHUNK, :])
    o_ref[...] = x + _rms(acc, g_ref[5:6, :])


def _post(layer, x, mix, kv, gains, w_out, wq, wo, w1, w2):
    B, S, D = x.shape
    tm = min(POST_TILE, S)
    w_out = w_out.astype(BF16)
    tile = lambda width: pl.BlockSpec((None, tm, width), lambda b, s: (b, s, 0))
    params = pltpu.CompilerParams(dimension_semantics=("arbitrary", "arbitrary"), vmem_limit_bytes=VMEM_LIMIT)
    x = pl.pallas_call(
        _attn_kernel,
        grid=(B, S // tm),
        in_specs=[
            tile(D), tile(mix.shape[-1]),
            pl.BlockSpec((None, None, kv.shape[2], kv.shape[3]), lambda b, s: (layer, b, 0, 0)),
            _layer_spec(gains.shape, layer), _const_spec(w_out.shape),
            _layer_spec(wq.shape, layer), _layer_spec(wo.shape, layer),
        ],
        out_specs=tile(D),
        out_shape=jax.ShapeDtypeStruct((B, S, D), F32),
        compiler_params=params,
        name="attn",
    )(x, mix, kv, gains, w_out, wq, wo)
    return pl.pallas_call(
        _mlp_kernel,
        grid=(B, S // tm),
        in_specs=[tile(D), _layer_spec(gains.shape, layer), _layer_spec(w1.shape, layer),
                  _layer_spec(w2.shape, layer)],
        out_specs=tile(D),
        out_shape=jax.ShapeDtypeStruct((B, S, D), F32),
        compiler_params=params,
        name="mlp",
    )(x, gains, w1, w2)


def kernel(x, mem, norm_gains, xa_wq, xa_wkv, xa_wo, mlp_w1, mlp_w2, ab_w_in, pool_w, pool_scale, ssm_conv_w, ssm_conv_b, ssm_dt_bias, ssm_a_log, ssm_d, ssm_norm, ab_w_out, cd_w_in, conf_dw_w, conf_dw_b, conf_ln_g, conf_ln_b, sc_conv_w, cd_w_out):
    depth = norm_gains.shape[0]
    gains = norm_gains.astype(F32)
    wq, wo, w1, w2 = (w.astype(BF16) for w in (xa_wq, xa_wo, mlp_w1, mlp_w2))
    kv = _kv_proj(mem, xa_wkv)
    for layer in range(depth):
        g = gains[layer]
        i = layer // 2
        if layer % 2 == 0:
            mix = _mixer_ab(x, g, ab_w_in[i], pool_w[i], pool_scale[i], ssm_conv_w[i], ssm_conv_b[i],
                            ssm_dt_bias[i], ssm_a_log[i], ssm_d[i], ssm_norm[i])
            w_out = ab_w_out[i]
        else:
            mix = _mixer_cd(x, g, cd_w_in[i], conf_dw_w[i], conf_dw_b[i], conf_ln_g[i], conf_ln_b[i],
                            sc_conv_w[i])
            w_out = cd_w_out[i]
        x = _post(layer, x, mix, kv, gains, w_out, wq, wo, w1, w2)
    return x
```

```python
import jax
import jax.numpy as jnp
from jax import lax
from jax.experimental import pallas as pl
from jax.experimental.pallas import tpu as pltpu

F32 = jnp.float32
BF16 = jnp.bfloat16

D_MODEL = 1024
N_MEM = 256
XA_HEADS = 4
XA_HEAD_DIM = D_MODEL // XA_HEADS

LANES = 128
MXU_N = 256

POOL_WINDOWS = (2, 4, 8, 16)
POOL_GROUP_DIM = 128
POOL_WIDTH = len(POOL_WINDOWS) * POOL_GROUP_DIM
POOL_HALO = 16

SSM_HEAD_DIM = 64
SSM_INNER = 1024
SSM_HEADS = 16
SSM_GROUPS = 2
SSM_STATE = 128
SSM_CONV = 4
SSM_CONV_HALO = 8
SSM_CHUNK = 128
SSM_GN = SSM_GROUPS * SSM_STATE
SSM_CONV_DIM = SSM_INNER + 2 * SSM_GN
AB_IN = POOL_WIDTH + SSM_INNER + SSM_CONV_DIM + SSM_HEADS
AB_OUT = POOL_WIDTH + SSM_INNER
AB_IN_PAD = POOL_WIDTH + SSM_INNER + SSM_CONV_DIM + LANES
OFF_Z = POOL_WIDTH
OFF_XBC = OFF_Z + SSM_INNER
OFF_DT = OFF_XBC + SSM_CONV_DIM

CONF_DIM = 1024
CONF_KERNEL = 31
CONF_HALO = 32
SC_DIM = 1024
SC_KERNEL = 3
SC_HALO = 8
CD_IN = 2 * CONF_DIM + 3 * SC_DIM
CD_OUT = CONF_DIM + SC_DIM
CONV_ROWS = 64
CD_BLOCKS_PER_GROUP = CD_IN // CONF_DIM

MLP_HIDDEN = 4096
MLP_CHUNK = 512
RMS_EPS = 1e-6
LN_EPS = 1e-5
LOG2_E = 1.4426950408889634

SEQ_TILE = 512
POST_TILE = 1024
ATTN_ROW_SPLIT = 2
MLP_ROW_SPLIT = 2
VMEM_LIMIT = 56 * 1024 * 1024


def _rms(x, g):
    return x * lax.rsqrt(jnp.mean(x * x, axis=-1, keepdims=True) + RMS_EPS) * g


def _sigmoid(x):
    return 0.5 * jnp.tanh(0.5 * x) + 0.5


def _silu(x):
    hx = 0.5 * x
    return hx * jnp.tanh(hx) + hx


def _dot(a, b):
    return jnp.dot(a, b, preferred_element_type=F32)


def _const_spec(shape):
    nd = len(shape)
    return pl.BlockSpec(shape, lambda *_: (0,) * nd, pipeline_mode=pl.Buffered(1))


def _layer_spec(stack_shape, layer):
    nd = len(stack_shape) - 1
    return pl.BlockSpec((None,) + tuple(stack_shape[1:]), lambda *_: (layer,) + (0,) * nd,
                        pipeline_mode=pl.Buffered(1))


def _store_slabs(slab_ref, first_slab, row0, val):
    rows = val.shape[0]
    for j in range(val.shape[1] // LANES):
        slab_ref[first_slab + j, row0:row0 + rows, :] = val[:, j * LANES:(j + 1) * LANES]


def _dwconv(slab_ref, slab, w_ref, cols, width, halo, row0, rows, init):
    acc = init
    for k in range(width):
        r = halo - (width - 1) + k + row0
        acc = acc + w_ref[k:k + 1, cols] * slab_ref[slab, r:r + rows, :]
    return acc


def _ab_kernel(x_ref, g_ref, win_ref, poolw_ref, pscale_ref, convw_ref, convb_ref, dtb_ref,
               alog_ref, dskip_ref, normw_ref, o_ref, pool_s, xc_s, z_s, state_s):
    tq = x_ref.shape[0]
    s_idx = pl.program_id(1)
    HP, HC, L = POOL_HALO, SSM_CONV_HALO, SSM_CHUNK
    n_chunks = tq // L
    x_slabs = SSM_INNER // LANES
    b_slab0 = x_slabs
    c_slab0 = x_slabs + SSM_GN // LANES

    @pl.when(s_idx == 0)
    def _():
        pool_s[:, 0:HP, :] = jnp.zeros((pool_s.shape[0], HP, LANES), F32)
        xc_s[:, 0:HC, :] = jnp.zeros((xc_s.shape[0], HC, LANES), F32)
        state_s[...] = jnp.zeros_like(state_s)

    h = _rms(x_ref[...], g_ref[0:1, :]).astype(BF16)

    def conv_silu(slab, row0):
        cols = slice(slab * LANES, (slab + 1) * LANES)
        return _silu(_dwconv(xc_s, slab, convw_ref, cols, SSM_CONV, HC, row0, L, convb_ref[:, cols]))

    _store_slabs(pool_s, 0, HP, _dot(h, win_ref[:, 0:POOL_WIDTH]))
    pos = (s_idx * tq + 1 + lax.broadcasted_iota(jnp.int32, (tq, 1), 0)).astype(F32)
    for g, w in enumerate(POOL_WINDOWS):
        c0 = g * POOL_GROUP_DIM
        cur = pool_s[g, HP:HP + tq, :]
        acc = cur
        for j in range(1, w):
            acc = acc + pool_s[g, HP - j:HP - j + tq, :]
        mean = acc / jnp.minimum(pos, float(w))
        po = _dot((mean - cur).astype(BF16), poolw_ref[g]) * pscale_ref[:, c0:c0 + POOL_GROUP_DIM]
        o_ref[:, c0:c0 + POOL_GROUP_DIM] = po.astype(o_ref.dtype)

    a_row = -jnp.exp(alog_ref[...])
    ri = lax.broadcasted_iota(jnp.int32, (L, L), 0)
    ci = lax.broadcasted_iota(jnp.int32, (L, L), 1)
    causal = ri >= ci
    tril = jnp.where(causal, 1.0, 0.0).astype(BF16)
    lo_half = lax.broadcasted_iota(jnp.int32, (1, LANES), 1) < SSM_HEAD_DIM
    dt_raw = _dot(h, win_ref[:, OFF_DT:OFF_DT + LANES])
    dtps, css, cs_ts = [], [], []
    for c in range(n_chunks):
        dtp = jax.nn.softplus(dt_raw[c * L:(c + 1) * L, :] + dtb_ref[...])
        d_a = dtp * a_row
        hi = d_a.astype(BF16)
        r1 = d_a - hi.astype(F32)
        mid = r1.astype(BF16)
        lo = (r1 - mid.astype(F32)).astype(BF16)
        cs = _dot(tril, hi) + _dot(tril, mid) + _dot(tril, lo)
        cs = cs * LOG2_E
        dtps.append(dtp)
        css.append(cs)
        cs_ts.append(cs.T)

    for c0 in range(0, SSM_CONV_DIM, MXU_N):
        _store_slabs(xc_s, c0 // LANES, HC, _dot(h, win_ref[:, OFF_XBC + c0:OFF_XBC + c0 + MXU_N]))
    for c0 in range(0, SSM_INNER, MXU_N):
        z_s[:, c0:c0 + MXU_N] = _dot(h, win_ref[:, OFF_Z + c0:OFF_Z + c0 + MXU_N])

    slabs_per_group = x_slabs // SSM_GROUPS
    group_width = SSM_INNER // SSM_GROUPS
    for c in range(n_chunks):
        rows = slice(c * L, (c + 1) * L)
        cs, cs_t, dtp = css[c], cs_ts[c], dtps[c]
        for g in range(SSM_GROUPS):
            gcols = slice(g * group_width, (g + 1) * group_width)
            bm = conv_silu(b_slab0 + g, c * L)
            cm_b = conv_silu(c_slab0 + g, c * L).astype(BF16)
            bm_t = bm.T.astype(BF16)
            cb = lax.dot_general(cm_b, bm.astype(BF16), (((1,), (1,)), ((), ())),
                                 preferred_element_type=F32)
            xs_l, xdt_l, cs_l, yd_l = [], [], [], []
            for sl in range(slabs_per_group):
                slab = g * slabs_per_group + sl
                h0 = 2 * slab
                xs = conv_silu(slab, c * L)
                col0 = jnp.broadcast_to(cs[:, h0:h0 + 1], (L, L))
                col1 = jnp.broadcast_to(cs[:, h0 + 1:h0 + 2], (L, L))
                row0 = jnp.broadcast_to(cs_t[h0:h0 + 1, :], (L, L))
                row1 = jnp.broadcast_to(cs_t[h0 + 1:h0 + 2, :], (L, L))
                dt_pair = jnp.where(lo_half, jnp.broadcast_to(dtp[:, h0:h0 + 1], (L, LANES)),
                                    jnp.broadcast_to(dtp[:, h0 + 1:h0 + 2], (L, LANES)))
                xdt = xs * dt_pair
                xdt_b = xdt.astype(BF16)
                zero_b = jnp.zeros_like(xdt_b)
                m0 = (cb * jnp.exp2(jnp.where(causal, col0 - row0, -jnp.inf))).astype(BF16)
                m1 = (cb * jnp.exp2(jnp.where(causal, col1 - row1, -jnp.inf))).astype(BF16)
                m01 = jnp.concatenate([m0, m1], axis=1)
                x01 = jnp.concatenate([jnp.where(lo_half, xdt_b, zero_b),
                                       jnp.where(lo_half, zero_b, xdt_b)], axis=0)
                yd_l.append(_dot(m01, x01))
                xs_l.append(xs)
                xdt_l.append(xdt)
                cs_l.append(jnp.where(lo_half, col0, col1))
            xs_g = jnp.concatenate(xs_l, axis=1)
            xdt_g = jnp.concatenate(xdt_l, axis=1)
            cs_g = jnp.concatenate(cs_l, axis=1)
            hs = state_s[:, gcols]
            y = jnp.concatenate(yd_l, axis=1) + _dot(cm_b, hs.astype(BF16)) * jnp.exp2(cs_g)
            last = cs_g[L - 1:L, :]
            xw = (xdt_g * jnp.exp2(last - cs_g)).astype(BF16)
            state_s[:, gcols] = hs * jnp.exp2(last) + _dot(bm_t, xw)
            y = y + dskip_ref[:, gcols] * xs_g
            y = y * _silu(z_s[rows, gcols])
            scale = lax.rsqrt(jnp.mean(y * y, axis=-1, keepdims=True) + RMS_EPS)
            yo = y * scale * normw_ref[:, gcols]
            o_ref[rows, POOL_WIDTH + g * group_width:POOL_WIDTH + (g + 1) * group_width] = yo.astype(o_ref.dtype)

    pool_s[:, 0:HP, :] = pool_s[:, tq:tq + HP, :]
    xc_s[:, 0:HC, :] = xc_s[:, tq:tq + HC, :]


def _mixer_ab(x, gains, w_in, pool_w, pool_scale, conv_w, conv_b, dt_bias, a_log, d_skip, norm_w):
    B, S, D = x.shape
    tq = min(SEQ_TILE, S)
    pad = AB_IN_PAD - AB_IN
    win = jnp.pad(w_in.astype(BF16), ((0, 0), (0, pad)))
    row = lambda v: v.reshape(1, -1).astype(F32)
    lane_pad = lambda v: jnp.pad(v.astype(F32), (0, LANES - v.shape[0])).reshape(1, LANES)
    args = (
        gains, win, pool_w.astype(BF16), row(pool_scale), conv_w.astype(F32), row(conv_b),
        lane_pad(dt_bias), lane_pad(a_log), row(jnp.repeat(d_skip, SSM_HEAD_DIM)), row(norm_w),
    )
    return pl.pallas_call(
        _ab_kernel,
        grid=(B, S // tq),
        in_specs=[pl.BlockSpec((None, tq, D), lambda b, s: (b, s, 0))] + [_const_spec(a.shape) for a in args],
        out_specs=pl.BlockSpec((None, tq, AB_OUT), lambda b, s: (b, s, 0)),
        out_shape=jax.ShapeDtypeStruct((B, S, AB_OUT), BF16),
        scratch_shapes=[
            pltpu.VMEM((POOL_WIDTH // LANES, POOL_HALO + tq, LANES), F32),
            pltpu.VMEM((SSM_CONV_DIM // LANES, SSM_CONV_HALO + tq, LANES), F32),
            pltpu.VMEM((tq, SSM_INNER), F32),
            pltpu.VMEM((SSM_STATE, SSM_INNER), F32),
        ],
        compiler_params=pltpu.CompilerParams(
            dimension_semantics=("arbitrary", "arbitrary"), vmem_limit_bytes=VMEM_LIMIT),
        name="mixer_ab",
    )(x, *args)


def _cd_kernel(x_ref, g_ref, win_ref, dww_ref, dwb_ref, lng_ref, lnb_ref, scw_ref, o_ref, *scratch):
    tq = x_ref.shape[0]
    s_idx = pl.program_id(1)
    HV, HP = CONF_HALO, SC_HALO
    slabs_per_col_group = MXU_N // LANES
    n_groups = CONF_DIM // MXU_N
    v_grp = scratch[0:n_groups]
    p_grp = scratch[n_groups:2 * n_groups]
    c_s, h_s = scratch[2 * n_groups:]

    @pl.when(s_idx == 0)
    def _():
        for v_s in v_grp:
            v_s[:, 0:HV, :] = jnp.zeros((slabs_per_col_group, HV, LANES), F32)
        for p_s in p_grp:
            p_s[:, 0:HP, :] = jnp.zeros((slabs_per_col_group, HP, LANES), F32)


    def project(cg, blk):
        base = (cg * CD_BLOCKS_PER_GROUP + blk) * MXU_N
        return _dot(h_s[...], win_ref[:, base:base + MXU_N])

    def project_glu(cg):
        for j in range(slabs_per_col_group):
            r = project(cg, j)
            v_grp[cg][j, HV:HV + tq, :] = r[:, 0:LANES] * _sigmoid(r[:, LANES:])

    def short_conv(cg):
        for j in range(slabs_per_col_group):
            r = project(cg, slabs_per_col_group + j)
            p_grp[cg][j, HP:HP + tq, :] = r[:, 0:LANES] * r[:, LANES:]
        bg = project(cg, 2 * slabs_per_col_group)
        for j in range(slabs_per_col_group):
            slab = cg * slabs_per_col_group + j
            cols = slice(slab * LANES, (slab + 1) * LANES)
            conv = _dwconv(p_grp[cg], j, scw_ref, cols, SC_KERNEL, HP, 0, tq, jnp.zeros((1, LANES), F32))
            o_ref[:, CONF_DIM + slab * LANES:CONF_DIM + (slab + 1) * LANES] = (
                bg[:, j * LANES:(j + 1) * LANES] * conv).astype(o_ref.dtype)

    def conv31_block(cg, blk):
        j, rb = divmod(blk, tq // CONV_ROWS)
        slab = cg * slabs_per_col_group + j
        cols = slice(slab * LANES, (slab + 1) * LANES)
        r0 = rb * CONV_ROWS
        c_s[r0:r0 + CONV_ROWS, cols] = _dwconv(v_grp[cg], j, dww_ref, cols, CONF_KERNEL, HV, r0,
                                               CONV_ROWS, dwb_ref[:, cols])

    blocks_per_group = slabs_per_col_group * (tq // CONV_ROWS)
    third = blocks_per_group // 3

    h_s[...] = _rms(x_ref[...], g_ref[0:1, :]).astype(BF16)
    project_glu(0)
    for cg in range(n_groups):
        for blk in range(0, third):
            conv31_block(cg, blk)
        if cg + 1 < n_groups:
            project_glu(cg + 1)
        for blk in range(third, 2 * third):
            conv31_block(cg, blk)
        short_conv(cg)
        for blk in range(2 * third, blocks_per_group):
            conv31_block(cg, blk)

    for rb in range(tq // CONV_ROWS):
        rows = slice(rb * CONV_ROWS, (rb + 1) * CONV_ROWS)
        acc = c_s[rows, :]
        mu = jnp.mean(acc, axis=-1, keepdims=True)
        cen = acc - mu
        var = jnp.mean(cen * cen, axis=-1, keepdims=True)
        vn = cen * lax.rsqrt(var + LN_EPS) * lng_ref[...] + lnb_ref[...]
        o_ref[rows, 0:CONF_DIM] = _silu(vn).astype(o_ref.dtype)

    for v_s in v_grp:
        v_s[:, 0:HV, :] = v_s[:, tq:tq + HV, :]
    for p_s in p_grp:
        p_s[:, 0:HP, :] = p_s[:, tq:tq + HP, :]


def _cd_weight_layout(w_in):
    val, gate, bg, cgate, hh = range(5)

    def tile(part, s):
        c0 = part * CONF_DIM + s * LANES
        return w_in[:, c0:c0 + LANES]

    cols = []
    for cg in range(CONF_DIM // MXU_N):
        a, b = 2 * cg, 2 * cg + 1
        cols += [tile(val, a), tile(gate, a), tile(val, b), tile(gate, b),
                 tile(cgate, a), tile(hh, a), tile(cgate, b), tile(hh, b), tile(bg, a), tile(bg, b)]
    return jnp.concatenate(cols, axis=1)


def _mixer_cd(x, gains, w_in, dw_w, dw_b, ln_g, ln_b, sc_w):
    B, S, D = x.shape
    tq = min(SEQ_TILE, S)
    row = lambda v: v.reshape(1, -1).astype(F32)
    args = (gains, _cd_weight_layout(w_in.astype(BF16)), dw_w.astype(F32), row(dw_b), row(ln_g), row(ln_b),
            sc_w.astype(F32))
    return pl.pallas_call(
        _cd_kernel,
        grid=(B, S // tq),
        in_specs=[pl.BlockSpec((None, tq, D), lambda b, s: (b, s, 0))] + [_const_spec(a.shape) for a in args],
        out_specs=pl.BlockSpec((None, tq, CD_OUT), lambda b, s: (b, s, 0)),
        out_shape=jax.ShapeDtypeStruct((B, S, CD_OUT), BF16),
        scratch_shapes=(
            [pltpu.VMEM((MXU_N // LANES, CONF_HALO + tq, LANES), F32)] * (CONF_DIM // MXU_N)
            + [pltpu.VMEM((MXU_N // LANES, SC_HALO + tq, LANES), F32)] * (SC_DIM // MXU_N)
            + [pltpu.VMEM((tq, CONF_DIM), F32), pltpu.VMEM((tq, D), BF16)]
        ),
        compiler_params=pltpu.CompilerParams(
            dimension_semantics=("arbitrary", "arbitrary"), vmem_limit_bytes=VMEM_LIMIT),
        name="mixer_cd",
    )(x, *args)


def _kv_kernel(mem_ref, wkv_ref, o_ref):
    o_ref[...] = _dot(mem_ref[...].astype(BF16), wkv_ref[...]).astype(o_ref.dtype)


def _kv_proj(mem, wkv):
    B, M, D = mem.shape
    depth = wkv.shape[0]
    rows = B * M
    tm = min(SEQ_TILE, rows)
    kv = pl.pallas_call(
        _kv_kernel,
        grid=(depth, rows // tm),
        in_specs=[pl.BlockSpec((tm, D), lambda l, r: (r, 0)),
                  pl.BlockSpec((None, D, 2 * D), lambda l, r: (l, 0, 0))],
        out_specs=pl.BlockSpec((None, tm, 2 * D), lambda l, r: (l, r, 0)),
        out_shape=jax.ShapeDtypeStruct((depth, rows, 2 * D), BF16),
        compiler_params=pltpu.CompilerParams(
            dimension_semantics=("arbitrary", "arbitrary"), vmem_limit_bytes=VMEM_LIMIT),
        name="kv_proj",
    )(mem.reshape(rows, D), wkv.astype(BF16))
    return kv.reshape(depth, B, M, 2 * D)


def _attn_kernel(x_ref, mix_ref, kv_ref, g_ref, wout_ref, wq_ref, wo_ref, o_ref):
    tm = x_ref.shape[0]
    half = tm // ATTN_ROW_SPLIT
    halves = [slice(i * half, (i + 1) * half) for i in range(ATTN_ROW_SPLIT)]

    def out_proj(r):
        return x_ref[r, :] + _rms(_dot(mix_ref[r, :], wout_ref[...]), g_ref[1:2, :])

    def q_proj(x):
        return _dot(_rms(x, g_ref[2:3, :]).astype(BF16), wq_ref[...])

    def attend(q):
        heads = []
        for hd in range(XA_HEADS):
            cols = slice(hd * XA_HEAD_DIM, (hd + 1) * XA_HEAD_DIM)
            kh = kv_ref[:, cols]
            vh = kv_ref[:, D_MODEL + hd * XA_HEAD_DIM:D_MODEL + (hd + 1) * XA_HEAD_DIM]
            s = lax.dot_general(q[:, cols].astype(BF16), kh, (((1,), (1,)), ((), ())),
                                preferred_element_type=F32) * (1.0 / XA_HEAD_DIM ** 0.5)
            e = jnp.exp(s - jnp.max(s, axis=-1, keepdims=True))
            p = e * (1.0 / jnp.sum(e, axis=-1, keepdims=True))
            heads.append(_dot(p.astype(BF16), vh).astype(BF16))
        return jnp.concatenate(heads, axis=-1)

    def finish(r, x, o):
        o_ref[r, :] = x + _rms(_dot(o, wo_ref[...]), g_ref[3:4, :])

    xs = [out_proj(r) for r in halves]
    qs = [q_proj(x) for x in xs]
    outs = [attend(q) for q in qs]
    for r, x, o in zip(halves, xs, outs):
        finish(r, x, o)


def _mlp_kernel(x_ref, g_ref, w1_ref, w2_ref, o_ref):
    tm = x_ref.shape[0]
    half = tm // MLP_ROW_SPLIT
    halves = [slice(i * half, (i + 1) * half) for i in range(MLP_ROW_SPLIT)]
    xs = [x_ref[r, :] for r in halves]
    hs = [_rms(x, g_ref[4:5, :]).astype(BF16) for x in xs]
    accs = [jnp.zeros((half, D_MODEL), F32) for _ in halves]
    for c in range(0, MLP_HIDDEN, MLP_CHUNK):
        for i, h in enumerate(hs):
            r = jnp.maximum(_dot(h, w1_ref[:, c:c + MLP_CHUNK]), 0.0)
            accs[i] = accs[i] + _dot((r * r).astype(BF16), w2_ref[c:c + MLP_CHUNK, :])
    for r, x, acc in zip(halves, xs, accs):
        o_ref[r, :] = x + _rms(acc, g_ref[5:6, :])


def _post(layer, x, mix, kv, gains, w_out, wq, wo, w1, w2):
    B, S, D = x.shape
    tm = min(POST_TILE, S)
    w_out = w_out.astype(BF16)
    tile = lambda width: pl.BlockSpec((None, tm, width), lambda b, s: (b, s, 0))
    params = pltpu.CompilerParams(dimension_semantics=("arbitrary", "arbitrary"), vmem_limit_bytes=VMEM_LIMIT)
    x = pl.pallas_call(
        _attn_kernel,
        grid=(B, S // tm),
        in_specs=[
            tile(D), tile(mix.shape[-1]),
            pl.BlockSpec((None, None, kv.shape[2], kv.shape[3]), lambda b, s: (layer, b, 0, 0)),
            _layer_spec(gains.shape, layer), _const_spec(w_out.shape),
            _layer_spec(wq.shape, layer), _layer_spec(wo.shape, layer),
        ],
        out_specs=tile(D),
        out_shape=jax.ShapeDtypeStruct((B, S, D), F32),
        compiler_params=params,
        name="attn",
    )(x, mix, kv, gains, w_out, wq, wo)
    return pl.pallas_call(
        _mlp_kernel,
        grid=(B, S // tm),
        in_specs=[tile(D), _layer_spec(gains.shape, layer), _layer_spec(w1.shape, layer),
                  _layer_spec(w2.shape, layer)],
        out_specs=tile(D),
        out_shape=jax.ShapeDtypeStruct((B, S, D), F32),
        compiler_params=params,
        name="mlp",
    )(x, gains, w1, w2)


def kernel(x, mem, norm_gains, xa_wq, xa_wkv, xa_wo, mlp_w1, mlp_w2, ab_w_in, pool_w, pool_scale, ssm_conv_w, ssm_conv_b, ssm_dt_bias, ssm_a_log, ssm_d, ssm_norm, ab_w_out, cd_w_in, conf_dw_w, conf_dw_b, conf_ln_g, conf_ln_b, sc_conv_w, cd_w_out):
    depth = norm_gains.shape[0]
    gains = norm_gains.astype(F32)
    wq, wo, w1, w2 = (w.astype(BF16) for w in (xa_wq, xa_wo, mlp_w1, mlp_w2))
    kv = _kv_proj(mem, xa_wkv)
    for layer in range(depth):
        g = gains[layer]
        i = layer // 2
        if layer % 2 == 0:
            mix = _mixer_ab(x, g, ab_w_in[i], pool_w[i], pool_scale[i], ssm_conv_w[i], ssm_conv_b[i],
                            ssm_dt_bias[i], ssm_a_log[i], ssm_d[i], ssm_norm[i])
            w_out = ab_w_out[i]
        else:
            mix = _mixer_cd(x, g, cd_w_in[i], conf_dw_w[i], conf_dw_b[i], conf_ln_g[i], conf_ln_b[i],
                            sc_conv_w[i])
            w_out = cd_w_out[i]
        x = _post(layer, x, mix, kv, gains, w_out, wq, wo, w1, w2)
    return x
```
